```python
import math
import jax
import jax.numpy as jnp
from jax import lax
import numpy as np

D_MODEL = 4096
BATCH = 4
SEQ = 2048
DEPTH = 2
DEC_BATCH = 128
DEC_SEQ = 1
PAST_LEN = 16384
PAGE_SIZE = 128

N_EVEN = (DEPTH + 1) // 2
N_ODD = DEPTH // 2
MIX_WIDTH = D_MODEL
A_WIDTH = MIX_WIDTH // 2
A_HEAD_DIM = 64
A_HEADS = A_WIDTH // A_HEAD_DIM
A_LORA_W = 64
A_LORA_A = 64
A_LORA_G = 128
A_COLS = 3 * A_WIDTH + A_LORA_W + A_LORA_A + A_LORA_G
A_LN_EPS = 64e-5
B_WIDTH = MIX_WIDTH - A_WIDTH
B_HEAD_DIM = 128
B_HEADS = B_WIDTH // B_HEAD_DIM
B_COLS = 4 * B_WIDTH + 2 * B_HEADS
B_CHUNK = 64
CONV_W = 4
C_WIDTH = MIX_WIDTH // 2
C_BLOCKS = 16
C_BLOCK_DIM = C_WIDTH // C_BLOCKS
C_COLS = 2 * C_WIDTH
RGLRU_C = 8.0
D_WIDTH = MIX_WIDTH - C_WIDTH
D_HEAD_DIM = 128
D_HEADS = D_WIDTH // D_HEAD_DIM
D_COLS = 4 * D_WIDTH
D_CHUNK = 16
EVEN_COLS = A_COLS + B_COLS
ODD_COLS = C_COLS + D_COLS
N_GROUPS = 8
EXPERTS_PER_GROUP = 8
N_EXPERTS = N_GROUPS * EXPERTS_PER_GROUP
TOP_K = 2
D_EXPERT = 1024
MOE_BLOCK = 128
NORM_EPS = 1e-6

kernel_name = 'hybrid_rwkv7_gdn_rglru_hgrn2_hmoe_step'

F32 = jnp.float32
EVEN_KEYS = ('w_in_even', 'w_out_even', 'a_mu', 'a_w0', 'a_w_up', 'a_a0', 'a_a_up', 'a_g_up', 'a_k_k', 'a_k_a',
             'a_r_k', 'a_ln_w', 'a_ln_b', 'b_conv_w', 'b_A_log', 'b_dt_bias', 'b_norm_w')
ODD_KEYS = ('w_in_odd', 'w_out_odd', 'c_conv_w', 'c_conv_b', 'c_w_r', 'c_b_r', 'c_w_i', 'c_b_i', 'c_lambda', 'd_norm_w')
STATE_KEYS = ('a_shift', 'a_wkv', 'b_conv', 'b_delta', 'c_conv', 'c_h', 'd_s')


def rmsnorm(x, w):
    xf = x.astype(F32)
    y = xf * lax.rsqrt(jnp.mean(xf * xf, axis=-1, keepdims=True) + NORM_EPS)
    return (y * w.astype(F32)).astype(x.dtype)


def l2norm(x):
    xf = x.astype(F32)
    return xf * lax.rsqrt(jnp.sum(xf * xf, axis=-1, keepdims=True) + NORM_EPS)


def causal_conv(x, buf, w):
    t = x.shape[1]
    xc = jnp.concatenate([buf.astype(x.dtype), x], axis=1)
    y = xc[:, 0:t] * w[0]
    for j in range(1, CONV_W):
        y = y + xc[:, j:j + t] * w[j]
    return y, xc[:, t:]


def to_chunks(u, chunk):
    t = u.shape[1]
    n_pad = (-t) % chunk
    u = jnp.pad(u, [(0, 0), (0, n_pad)] + [(0, 0)] * (u.ndim - 2))
    u = u.reshape((u.shape[0], -1, chunk) + u.shape[2:])
    return jnp.swapaxes(jnp.moveaxis(u, 1, 0), 2, 3)


def from_chunks(o, t):
    o = jnp.moveaxis(jnp.swapaxes(o, 2, 3), 0, 1)
    return o.reshape((o.shape[0], -1) + o.shape[3:])[:, :t]


def linear_scan_combine(left, right):
    a1, b1 = left
    a2, b2 = right
    return a1 * a2, a2 * b1 + b2


def rwkv7_mixer(z, shift, s0, pl):
    nb, t, _ = z.shape
    prev = jnp.concatenate([shift[:, None].astype(z.dtype), z[:, :-1]], axis=1)
    zs = z + (prev - z) * pl['a_mu']
    r, k, v, xw, xa, xg = jnp.split(zs, [A_WIDTH, 2 * A_WIDTH, 3 * A_WIDTH, 3 * A_WIDTH + A_LORA_W,
                                         3 * A_WIDTH + A_LORA_W + A_LORA_A], axis=-1)
    w = pl['a_w0'] + jnp.tanh(xw) @ pl['a_w_up']
    w = -jax.nn.softplus(-w.astype(F32)) - 0.5
    decay = jnp.exp(-jnp.exp(w))
    a = jax.nn.sigmoid((pl['a_a0'] + xa @ pl['a_a_up']).astype(F32))
    g = jax.nn.sigmoid(xg) @ pl['a_g_up']
    heads = lambda u: u.reshape(nb, t, A_HEADS, A_HEAD_DIM).astype(F32)
    kk = l2norm(heads(k * pl['a_k_k']))
    k = k.astype(F32) * (1.0 + (a - 1.0) * pl['a_k_a'].astype(F32))
    rh, kh, vh, wh, ah = heads(r), heads(k), heads(v), heads(decay), heads(a)

    def step(s, inp):
        r_t, w_t, k_t, v_t, kk_t, b_t = inp
        sa = jnp.einsum('bhvk,bhk->bhv', s, kk_t)
        s = s * w_t[:, :, None, :] - sa[..., None] * b_t[:, :, None, :] + v_t[..., None] * k_t[:, :, None, :]
        return s, jnp.einsum('bhvk,bhk->bhv', s, r_t)

    xs = tuple(jnp.moveaxis(u, 1, 0) for u in (rh, wh, kh, vh, kk, kk * ah))
    s_new, y = lax.scan(step, s0.astype(F32), xs)
    y = jnp.moveaxis(y, 0, 1)
    mu = jnp.mean(y, axis=-1, keepdims=True)
    var = jnp.mean(jnp.square(y - mu), axis=-1, keepdims=True)
    yn = ((y - mu) * lax.rsqrt(var + A_LN_EPS)).reshape(nb, t, A_WIDTH)
    yn = yn * pl['a_ln_w'].astype(F32) + pl['a_ln_b'].astype(F32)
    bonus = jnp.sum(rh * kh * pl['a_r_k'].astype(F32), axis=-1, keepdims=True) * vh
    out = (yn + bonus.reshape(nb, t, A_WIDTH)) * g.astype(F32)
    return out.astype(z.dtype), z[:, -1], s_new


def chunked_gated_delta(q, k, v, beta, g, s0):
    t = q.shape[1]
    chunk = min(B_CHUNK, t)
    qc, kc, vc, bc, gc = (to_chunks(u, chunk) for u in (q, k, v, beta, g))
    causal = jnp.tril(jnp.ones((chunk, chunk), dtype=bool))
    strict = jnp.tril(jnp.ones((chunk, chunk), dtype=bool), k=-1)
    cum = jnp.cumsum(gc, axis=-1)
    diff = cum[..., :, None] - cum[..., None, :]
    dmat = jnp.where(causal, jnp.exp(jnp.where(causal, diff, 0.0)), 0.0)
    lower = jnp.where(strict, bc[..., None] * dmat * jnp.einsum('nbhtk,nbhjk->nbhtj', kc, kc), 0.0)
    rhs = jnp.concatenate([bc[..., None] * vc, (bc * jnp.exp(cum))[..., None] * kc], axis=-1)
    sol = lax.linalg.triangular_solve(lower + jnp.eye(chunk, dtype=F32), rhs, left_side=True, lower=True,
                                      unit_diagonal=True)
    u_v, w_k = sol[..., :B_HEAD_DIM], sol[..., B_HEAD_DIM:]
    a_qk = dmat * jnp.einsum('nbhtk,nbhjk->nbhtj', qc, kc)
    q_dec = jnp.exp(cum)[..., None] * qc
    cum_end = cum[..., -1]
    k_dec = jnp.exp(cum_end[..., None] - cum)[..., None] * kc

    def step(s, inp):
        uv_c, wk_c, qd_c, aqk_c, kd_c, ce_c = inp
        u = uv_c - jnp.einsum('bhck,bhvk->bhcv', wk_c, s)
        o = jnp.einsum('bhck,bhvk->bhcv', qd_c, s) + jnp.einsum('bhcj,bhjv->bhcv', aqk_c, u)
        s = s * jnp.exp(ce_c)[..., None, None] + jnp.einsum('bhcv,bhck->bhvk', u, kd_c)
        return s, o

    s_new, o = lax.scan(step, s0.astype(F32), (u_v, w_k, q_dec, a_qk, k_dec, cum_end))
    return from_chunks(o, t), s_new


def gated_delta_mixer(z, conv_buf, s0, pl):
    nb, t, _ = z.shape
    qkv, zg, beta_in, alpha_in = jnp.split(z, [3 * B_WIDTH, 4 * B_WIDTH, 4 * B_WIDTH + B_HEADS], axis=-1)
    qkv, new_buf = causal_conv(qkv, conv_buf, pl['b_conv_w'])
    q, k, v = jnp.split(jax.nn.silu(qkv), 3, axis=-1)
    heads = lambda u: u.reshape(nb, t, B_HEADS, B_HEAD_DIM)
    q = l2norm(heads(q)) * (B_HEAD_DIM ** -0.5)
    k = l2norm(heads(k))
    v = heads(v).astype(F32)
    beta = jax.nn.sigmoid(beta_in.astype(F32))
    g = -jnp.exp(pl['b_A_log'].astype(F32)) * jax.nn.softplus(alpha_in.astype(F32) + pl['b_dt_bias'].astype(F32))
    o, s_new = chunked_gated_delta(q, k, v, beta, g, s0)
    o = rmsnorm(o, pl['b_norm_w']) * jax.nn.silu(heads(zg).astype(F32))
    return o.reshape(nb, t, B_WIDTH).astype(z.dtype), new_buf, s_new


def rglru_mixer(z, conv_buf, h0, pl):
    nb, t, _ = z.shape
    xin, gate = jnp.split(z, 2, axis=-1)
    xc, new_buf = causal_conv(xin, conv_buf, pl['c_conv_w'])
    xc = (xc + pl['c_conv_b']).astype(F32)
    xb = xc.reshape(nb, t, C_BLOCKS, C_BLOCK_DIM)
    blockdiag = lambda w, b: jnp.einsum('btni,nio->btno', xb, w.astype(F32)).reshape(nb, t, C_WIDTH) + b.astype(F32)
    r = jax.nn.sigmoid(blockdiag(pl['c_w_r'], pl['c_b_r']))
    i = jax.nn.sigmoid(blockdiag(pl['c_w_i'], pl['c_b_i']))
    log_a = -RGLRU_C * r * jax.nn.softplus(-pl['c_lambda'].astype(F32))
    a = jnp.exp(log_a)
    b = jnp.sqrt(-jnp.expm1(2.0 * log_a)) * i * xc
    a_cum, b_cum = lax.associative_scan(linear_scan_combine, (a, b), axis=1)
    h = a_cum * h0[:, None].astype(F32) + b_cum
    y = h * jax.nn.gelu(gate.astype(F32), approximate=True)
    return y.astype(z.dtype), new_buf, h[:, -1]


def chunked_hgrn2(q, k, v, log_f, s0):
    t = q.shape[1]
    chunk = min(D_CHUNK, t)
    qc, kc, vc, gc = (to_chunks(u, chunk) for u in (q, k, v, log_f))
    causal = jnp.tril(jnp.ones((chunk, chunk), dtype=bool))[:, :, None]

    def step(s, inp):
        q_c, k_c, v_c, g_c = inp
        cum = jnp.cumsum(g_c, axis=-2)
        diff = cum[..., :, None, :] - cum[..., None, :, :]
        dmat = jnp.where(causal, jnp.exp(jnp.where(causal, diff, 0.0)), 0.0)
        att = jnp.einsum('bhtc,bhjc,bhtjc->bhtj', q_c, k_c, dmat)
        o = jnp.einsum('bhtc,bhcv->bhtv', q_c * jnp.exp(cum), s) + jnp.einsum('bhtj,bhjv->bhtv', att, v_c)
        cum_end = cum[..., -1, :]
        s = s * jnp.exp(cum_end)[..., None] + jnp.einsum('bhjc,bhjv->bhcv', k_c * jnp.exp(cum_end[..., None, :] - cum), v_c)
        return s, o

    s_new, o = lax.scan(step, s0.astype(F32), (qc, kc, vc, gc))
    return from_chunks(o, t), s_new


def hgrn2_mixer(z, s0, lb, pl):
    nb, t, _ = z.shape
    q, f, i, og = jnp.split(z, 4, axis=-1)
    heads = lambda u: u.reshape(nb, t, D_HEADS, D_HEAD_DIM).astype(F32)
    lbh = lb.reshape(D_HEADS, D_HEAD_DIM)
    fg = lbh + (1.0 - lbh) * jax.nn.sigmoid(heads(f))
    o, s_new = chunked_hgrn2(jax.nn.silu(heads(q)), 1.0 - fg, heads(i), jnp.log(fg), s0)
    o = rmsnorm(o, pl['d_norm_w']) * jax.nn.silu(heads(og))
    return o.reshape(nb, t, D_WIDTH).astype(z.dtype), s_new


def moe_dispatch(hf, expert, gate, w_gate, w_up, w_down):
    n_tok, d = hf.shape
    n_rows = n_tok * TOP_K
    n_blocks = -(-n_rows // MOE_BLOCK) + N_EXPERTS
    total = n_blocks * MOE_BLOCK
    flat_e = expert.reshape(-1)
    flat_tok = jnp.repeat(jnp.arange(n_tok, dtype=jnp.int32), TOP_K)
    flat_w = gate.reshape(-1)
    order = jnp.argsort(flat_e)
    se, stok, sw = flat_e[order], flat_tok[order], flat_w[order]
    counts = jnp.bincount(flat_e, length=N_EXPERTS)
    start = jnp.cumsum(counts) - counts
    padded = (counts + MOE_BLOCK - 1) // MOE_BLOCK * MOE_BLOCK
    pad_end = jnp.cumsum(padded)
    pad_start = pad_end - padded
    dest = pad_start[se] + jnp.arange(n_rows, dtype=jnp.int32) - start[se]
    row_tok = jnp.full((total,), n_tok, jnp.int32).at[dest].set(stok)
    row_w = jnp.zeros((total,), F32).at[dest].set(sw)
    block_e = jnp.minimum(jnp.searchsorted(pad_end, jnp.arange(n_blocks, dtype=jnp.int32) * MOE_BLOCK, side='right'),
                          N_EXPERTS - 1)
    h_pad = jnp.concatenate([hf, jnp.zeros((1, d), hf.dtype)], axis=0)

    def expert_block(args):
        tok, e = args
        xb = h_pad[tok]
        return (jax.nn.silu(xb @ w_gate[e]) * (xb @ w_up[e])) @ w_down[e]

    ys = lax.map(expert_block, (row_tok.reshape(n_blocks, MOE_BLOCK), block_e))
    ys = ys.reshape(total, d) * row_w[:, None].astype(ys.dtype)
    return jnp.zeros((n_tok + 1, d), ys.dtype).at[row_tok].add(ys)[:n_tok]


def hier_moe(h, w_rg, b_rg, w_re, b_re, w_gate, w_up, w_down):
    nb, t, d = h.shape
    hf = h.reshape(nb * t, d)
    g_prob = jax.nn.softmax((hf @ w_rg + b_rg).astype(F32), axis=-1)
    g_top, g_idx = lax.top_k(g_prob, 1)
    e_logits = (hf @ w_re + b_re).astype(F32).reshape(-1, N_GROUPS, EXPERTS_PER_GROUP)
    sel = jnp.broadcast_to(g_idx[:, :, None], (hf.shape[0], 1, EXPERTS_PER_GROUP))
    e_prob = jax.nn.softmax(jnp.take_along_axis(e_logits, sel, axis=1)[:, 0], axis=-1)
    e_top, e_idx = lax.top_k(e_prob, TOP_K)
    gate = g_top * e_top / jnp.sum(e_top, axis=-1, keepdims=True)
    expert = g_idx * EXPERTS_PER_GROUP + e_idx
    out = moe_dispatch(hf, expert, gate, w_gate, w_up, w_down)
    return out.reshape(nb, t, d).astype(h.dtype)


def trunk(x, c, st, p):
    new = {k: [] for k in STATE_KEYS}
    sm = jax.nn.softmax(p['d_lb'].astype(F32), axis=0)
    lower_bounds = jnp.cumsum(sm, axis=0) - sm[0]
    for layer in range(DEPTH):
        mod = (jax.nn.silu(c) @ p['w_mod'][layer] + p['b_mod'][layer])[:, None, :]
        sh1, sc1, gt1, sh2, sc2, gt2 = jnp.split(mod, 6, axis=-1)
        h = rmsnorm(x, p['norm_mix'][layer]) * (1 + sc1) + sh1
        li = layer // 2
        if layer % 2 == 0:
            pl = {k: p[k][li] for k in EVEN_KEYS}
            zp = h @ pl['w_in_even']
            oa, a_shift, a_wkv = rwkv7_mixer(zp[..., :A_COLS], st['a_shift'][li], st['a_wkv'][li], pl)
            ob, b_conv, b_delta = gated_delta_mixer(zp[..., A_COLS:], st['b_conv'][li], st['b_delta'][li], pl)
            o = jnp.concatenate([oa, ob], axis=-1) @ pl['w_out_even']
            new['a_shift'].append(a_shift)
            new['a_wkv'].append(a_wkv)
            new['b_conv'].append(b_conv)
            new['b_delta'].append(b_delta)
        else:
            pl = {k: p[k][li] for k in ODD_KEYS}
            zp = h @ pl['w_in_odd']
            oc, c_conv, c_h = rglru_mixer(zp[..., :C_COLS], st['c_conv'][li], st['c_h'][li], pl)
            od, d_s = hgrn2_mixer(zp[..., C_COLS:], st['d_s'][li], lower_bounds[layer], pl)
            o = jnp.concatenate([oc, od], axis=-1) @ pl['w_out_odd']
            new['c_conv'].append(c_conv)
            new['c_h'].append(c_h)
            new['d_s'].append(d_s)
        x = x + gt1 * o
        h = rmsnorm(x, p['norm_ffn'][layer]) * (1 + sc2) + sh2
        x = x + gt2 * hier_moe(h, p['w_router_group'][layer], p['b_router_group'][layer],
                               p['w_router_expert'][layer], p['b_router_expert'][layer],
                               p['w_expert_gate'][layer], p['w_expert_up'][layer], p['w_expert_down'][layer])
    y = rmsnorm(x, p['final_norm'])
    return y, {k: jnp.stack(v, axis=0).astype(st[k].dtype) for k, v in new.items()}


def setup_inputs(seed: int = 0) -> dict:
    key = jax.random.key(seed)
    keys = iter(jax.random.split(key, 64))

    def nrm(shape, scale):
        return jax.random.normal(next(keys), shape, F32) * scale

    def uni(shape, lo, hi):
        return jax.random.uniform(next(keys), shape, F32, lo, hi)

    def gain(shape):
        return 1.0 + nrm(shape, 0.05)

    dt = jnp.exp(uni((N_EVEN, B_HEADS), math.log(1e-3), math.log(1e-1)))
    a_base = uni((N_ODD, C_WIDTH), 0.9, 0.999)
    return {
        'x_prompt': nrm((BATCH, SEQ, D_MODEL), 1.0),
        'x_sample': nrm((DEC_BATCH, DEC_SEQ, D_MODEL), 1.0),
        'c_prompt': nrm((BATCH, D_MODEL), 1.0),
        'c_sample': nrm((DEC_BATCH, D_MODEL), 1.0),
        'state_a_shift': nrm((N_EVEN, DEC_BATCH, A_COLS), 1.0),
        'state_a_wkv': nrm((N_EVEN, DEC_BATCH, A_HEADS, A_HEAD_DIM, A_HEAD_DIM), 0.2),
        'state_b_conv': nrm((N_EVEN, DEC_BATCH, CONV_W - 1, 3 * B_WIDTH), 1.0),
        'state_b_delta': nrm((N_EVEN, DEC_BATCH, B_HEADS, B_HEAD_DIM, B_HEAD_DIM), 0.1),
        'state_c_conv': nrm((N_ODD, DEC_BATCH, CONV_W - 1, C_WIDTH), 1.0),
        'state_c_h': nrm((N_ODD, DEC_BATCH, C_WIDTH), 0.5),
        'state_d_s': nrm((N_ODD, DEC_BATCH, D_HEADS, D_HEAD_DIM, D_HEAD_DIM), 0.5),
        'w_mod': nrm((DEPTH, D_MODEL, 6 * D_MODEL), 0.5 * D_MODEL ** -0.5),
        'b_mod': nrm((DEPTH, 6 * D_MODEL), 0.01),
        'norm_mix': gain((DEPTH, D_MODEL)),
        'norm_ffn': gain((DEPTH, D_MODEL)),
        'final_norm': gain((D_MODEL,)),
        'w_in_even': nrm((N_EVEN, D_MODEL, EVEN_COLS), D_MODEL ** -0.5),
        'w_out_even': nrm((N_EVEN, MIX_WIDTH, D_MODEL), MIX_WIDTH ** -0.5),
        'a_mu': uni((N_EVEN, A_COLS), 0.0, 1.0),
        'a_w0': uni((N_EVEN, A_WIDTH), -6.0, 1.0),
        'a_w_up': nrm((N_EVEN, A_LORA_W, A_WIDTH), 0.1 * A_LORA_W ** -0.5),
        'a_a0': nrm((N_EVEN, A_WIDTH), 0.1),
        'a_a_up': nrm((N_EVEN, A_LORA_A, A_WIDTH), 0.1 * A_LORA_A ** -0.5),
        'a_g_up': nrm((N_EVEN, A_LORA_G, A_WIDTH), A_LORA_G ** -0.5),
        'a_k_k': 0.85 + nrm((N_EVEN, A_WIDTH), 0.05),
        'a_k_a': gain((N_EVEN, A_WIDTH)),
        'a_r_k': nrm((N_EVEN, A_HEADS, A_HEAD_DIM), 0.1),
        'a_ln_w': gain((N_EVEN, A_WIDTH)),
        'a_ln_b': nrm((N_EVEN, A_WIDTH), 0.01),
        'b_conv_w': nrm((N_EVEN, CONV_W, 3 * B_WIDTH), CONV_W ** -0.5),
        'b_A_log': jnp.log(uni((N_EVEN, B_HEADS), 1.0, 16.0)),
        'b_dt_bias': dt + jnp.log(-jnp.expm1(-dt)),
        'b_norm_w': gain((N_EVEN, B_HEAD_DIM)),
        'w_in_odd': nrm((N_ODD, D_MODEL, ODD_COLS), D_MODEL ** -0.5),
        'w_out_odd': nrm((N_ODD, MIX_WIDTH, D_MODEL), MIX_WIDTH ** -0.5),
        'c_conv_w': nrm((N_ODD, CONV_W, C_WIDTH), CONV_W ** -0.5),
        'c_conv_b': nrm((N_ODD, C_WIDTH), 0.01),
        'c_w_r': nrm((N_ODD, C_BLOCKS, C_BLOCK_DIM, C_BLOCK_DIM), C_BLOCK_DIM ** -0.5),
        'c_b_r': nrm((N_ODD, C_WIDTH), 0.01),
        'c_w_i': nrm((N_ODD, C_BLOCKS, C_BLOCK_DIM, C_BLOCK_DIM), C_BLOCK_DIM ** -0.5),
        'c_b_i': nrm((N_ODD, C_WIDTH), 0.01),
        'c_lambda': jnp.log(a_base) - jnp.log1p(-a_base),
        'd_lb': nrm((DEPTH, D_WIDTH), 1.0),
        'd_norm_w': gain((N_ODD, D_HEAD_DIM)),
        'w_router_group': nrm((DEPTH, D_MODEL, N_GROUPS), D_MODEL ** -0.5),
        'b_router_group': nrm((DEPTH, N_GROUPS), 0.01),
        'w_router_expert': nrm((DEPTH, D_MODEL, N_EXPERTS), D_MODEL ** -0.5),
        'b_router_expert': nrm((DEPTH, N_EXPERTS), 0.01),
        'w_expert_gate': nrm((DEPTH, N_EXPERTS, D_MODEL, D_EXPERT), D_MODEL ** -0.5),
        'w_expert_up': nrm((DEPTH, N_EXPERTS, D_MODEL, D_EXPERT), D_MODEL ** -0.5),
        'w_expert_down': nrm((DEPTH, N_EXPERTS, D_EXPERT, D_MODEL), D_EXPERT ** -0.5),
    }


def reference(x_prompt, x_sample, c_prompt, c_sample, state_a_shift, state_a_wkv, state_b_conv, state_b_delta,
              state_c_conv, state_c_h, state_d_s, w_mod, b_mod, norm_mix, norm_ffn, final_norm, w_in_even, w_out_even,
              a_mu, a_w0, a_w_up, a_a0, a_a_up, a_g_up, a_k_k, a_k_a, a_r_k, a_ln_w, a_ln_b, b_conv_w, b_A_log,
              b_dt_bias, b_norm_w, w_in_odd, w_out_odd, c_conv_w, c_conv_b, c_w_r, c_b_r, c_w_i, c_b_i, c_lambda,
              d_lb, d_norm_w, w_router_group, b_router_group, w_router_expert, b_router_expert, w_expert_gate,
              w_expert_up, w_expert_down):
    p = dict(w_mod=w_mod, b_mod=b_mod, norm_mix=norm_mix, norm_ffn=norm_ffn, final_norm=final_norm,
             w_in_even=w_in_even, w_out_even=w_out_even, a_mu=a_mu, a_w0=a_w0, a_w_up=a_w_up, a_a0=a_a0,
             a_a_up=a_a_up, a_g_up=a_g_up, a_k_k=a_k_k, a_k_a=a_k_a, a_r_k=a_r_k, a_ln_w=a_ln_w, a_ln_b=a_ln_b,
             b_conv_w=b_conv_w, b_A_log=b_A_log, b_dt_bias=b_dt_bias, b_norm_w=b_norm_w, w_in_odd=w_in_odd,
             w_out_odd=w_out_odd, c_conv_w=c_conv_w, c_conv_b=c_conv_b, c_w_r=c_w_r, c_b_r=c_b_r, c_w_i=c_w_i,
             c_b_i=c_b_i, c_lambda=c_lambda, d_lb=d_lb, d_norm_w=d_norm_w, w_router_group=w_router_group,
             b_router_group=b_router_group, w_router_expert=w_router_expert, b_router_expert=b_router_expert,
             w_expert_gate=w_expert_gate, w_expert_up=w_expert_up, w_expert_down=w_expert_down)
    st_s = dict(a_shift=state_a_shift, a_wkv=state_a_wkv, b_conv=state_b_conv, b_delta=state_b_delta,
                c_conv=state_c_conv, c_h=state_c_h, d_s=state_d_s)
    nb = x_prompt.shape[0]
    zero = lambda s: jnp.zeros((s.shape[0], nb) + s.shape[2:], s.dtype)
    st_p = {k: zero(v) for k, v in st_s.items()}
    y_prompt, new_p = trunk(x_prompt, c_prompt, st_p, p)
    y_sample, new_s = trunk(x_sample, c_sample, st_s, p)
    return (y_prompt, y_sample,
            new_p['a_shift'], new_p['a_wkv'], new_p['b_conv'], new_p['b_delta'], new_p['c_conv'], new_p['c_h'], new_p['d_s'],
            new_s['a_shift'], new_s['a_wkv'], new_s['b_conv'], new_s['b_delta'], new_s['c_conv'], new_s['c_h'], new_s['d_s'])
```

```python
import functools

import jax
import jax.numpy as jnp
from jax import lax
from jax.experimental import pallas as pl
from jax.experimental.pallas import tpu as pltpu

F32 = jnp.float32
BF16 = jnp.bfloat16

D_MODEL = 4096
BATCH = 4
SEQ = 2048
DEPTH = 2
DEC_BATCH = 128
NP_ROWS = BATCH * SEQ
N_ROWS = NP_ROWS + DEC_BATCH
N_COND = DEC_BATCH + BATCH

A_WIDTH = 2048
A_HEAD = 64
A_HEADS = 32
A_LORA = 64
A_COLS = 3 * A_WIDTH + 64 + 64 + 128
A_LN_EPS = 64e-5
B_WIDTH = 2048
B_HEAD = 128
B_HEADS = 16
B_COLS = 4 * B_WIDTH + 2 * B_HEADS
CONV_W = 4
C_WIDTH = 2048
C_BLOCKS = 16
C_COLS = 2 * C_WIDTH
RGLRU_C = 8.0
D_WIDTH = 2048
D_HEADS = 16
D_COLS = 4 * D_WIDTH
EVEN_COLS = A_COLS + B_COLS
ODD_COLS = C_COLS + D_COLS
N_GROUPS = 8
EXPERTS_PER_GROUP = 8
N_EXPERTS = 64
D_EXPERT = 1024
NORM_EPS = 1e-6

LANES = 128
ROW_TILE = 128
N_ROW_TILES = N_ROWS // ROW_TILE
NP_ROW_TILES = NP_ROWS // ROW_TILE
MM_TM = 640
MM_TN = 512
A_CHUNK = 64
B_CHUNK = 64
D_SUB = 16
D_TILE = 128
C_TILE = 256
MOE_TM = 256
MOE_FC = 256
MOE_TILES = (2 * N_ROWS + MOE_TM - 1) // MOE_TM + N_EXPERTS
VMEM_LIMIT = 56 * 1024 * 1024


def _cparams(sem):
    return pltpu.CompilerParams(dimension_semantics=sem, vmem_limit_bytes=VMEM_LIMIT)


def _iota(shape, dim):
    return lax.broadcasted_iota(jnp.int32, shape, dim)


def _bdot(a, b):
    return jnp.dot(a.astype(BF16), b.astype(BF16), preferred_element_type=F32)


def _bdot_nt(a, b):
    return lax.dot_general(a.astype(BF16), b.astype(BF16), (((1,), (1,)), ((), ())),
                           preferred_element_type=F32)


def _bdot_tn(a, b):
    return lax.dot_general(a.astype(BF16), b.astype(BF16), (((0,), (0,)), ((), ())),
                           preferred_element_type=F32)


def _fdot_nt(a, b):
    return lax.dot_general(a, b, (((1,), (1,)), ((), ())), precision=lax.Precision.HIGHEST,
                           preferred_element_type=F32)


def _sigmoid(x):
    return 1.0 / (1.0 + jnp.exp(-x))


def _silu(x):
    return x * _sigmoid(x)


def _softplus(x):
    return jnp.maximum(x, 0.0) + jnp.log(1.0 + jnp.exp(-jnp.abs(x)))


def _gelu_tanh(x):
    return 0.5 * x * (1.0 + jnp.tanh(0.7978845608028654 * (x + 0.044715 * x * x * x)))


def _cumsum_rows(x, seg):
    pos = _iota(x.shape, 0) & (seg - 1)
    s = 1
    while s < seg:
        x = x + jnp.where(pos >= s, pltpu.roll(x, s, 0), 0.0)
        s *= 2
    return x


def _unit_lower_inverse(low, n, top):
    row = _iota((n, n), 0)
    col = _iota((n, n), 1)
    x = jnp.where(row == col, 1.0, 0.0)
    s = 1
    while s < top:
        sh = s.bit_length() - 1
        sel = (((row >> (sh + 1)) == (col >> (sh + 1))) & (((row >> sh) & 1) == 1)
               & (((col >> sh) & 1) == 0))
        x = x - _bdot(_bdot(x, jnp.where(sel, low, 0.0)), x)
        s *= 2
    return x


def _half_sum(x, lo):
    s_lo = jnp.sum(jnp.where(lo, x, 0.0), axis=-1, keepdims=True)
    s_hi = jnp.sum(jnp.where(lo, 0.0, x), axis=-1, keepdims=True)
    return jnp.where(lo, s_lo, s_hi)


def _row_to_col(row, n):
    eye = _iota((n, n), 0) == _iota((n, n), 1)
    return jnp.sum(jnp.where(eye, jnp.broadcast_to(row, (n, n)), 0.0), axis=1, keepdims=True)


def _col_to_row(col, n):
    eye = _iota((n, n), 0) == _iota((n, n), 1)
    return jnp.sum(jnp.where(eye, jnp.broadcast_to(col, (n, n)), 0.0), axis=0, keepdims=True)


def _mm_kernel(*refs, k_sizes, silu_in, has_bias):
    n_parts = len(k_sizes)
    x_refs = refs[:n_parts]
    w_ref = refs[n_parts]
    b_ref = refs[n_parts + 1] if has_bias else None
    o_ref = refs[n_parts + 1 + int(has_bias)]
    wb_ref = refs[n_parts + 2 + int(has_bias)]

    @pl.when(pl.program_id(1) == 0)
    def _():
        wb_ref[...] = w_ref[...].astype(BF16)

    acc = None
    off = 0
    for p in range(n_parts):
        x = x_refs[p][...]
        if silu_in:
            x = _silu(x.astype(F32))
        part = jnp.dot(x.astype(BF16), wb_ref[off:off + k_sizes[p], :], preferred_element_type=F32)
        acc = part if acc is None else acc + part
        off += k_sizes[p]
    if has_bias:
        acc = acc + b_ref[...]
    o_ref[...] = acc


def _matmul(parts, w, layer, bias=None, silu_in=False, tm=MM_TM, tn=MM_TN, name="proj"):
    m = parts[0].shape[0]
    k_sizes = tuple(p.shape[1] for p in parts)
    k_total = sum(k_sizes)
    n = w.shape[2]
    assert w.shape[1] == k_total and m % tm == 0
    grid = (pl.cdiv(n, tn), m // tm)
    in_specs = [pl.BlockSpec((tm, k), lambda j, i: (i, 0)) for k in k_sizes]
    in_specs.append(pl.BlockSpec((None, k_total, tn), lambda j, i: (layer, 0, j)))
    args = list(parts) + [w]
    if bias is not None:
        in_specs.append(pl.BlockSpec((None, 1, tn), lambda j, i: (layer, 0, j)))
        args.append(bias.reshape(bias.shape[0], 1, n))
    return pl.pallas_call(
        functools.partial(_mm_kernel, k_sizes=k_sizes, silu_in=silu_in, has_bias=bias is not None),
        grid=grid,
        in_specs=in_specs,
        out_specs=pl.BlockSpec((tm, tn), lambda j, i: (i, j)),
        out_shape=jax.ShapeDtypeStruct((m, n), F32),
        scratch_shapes=[pltpu.VMEM((k_total, tn), BF16)],
        compiler_params=_cparams(("arbitrary", "arbitrary")),
        name=name,
    )(*args)


def _rms(x, w):
    return x * lax.rsqrt(jnp.mean(x * x, axis=-1, keepdims=True) + NORM_EPS) * w


def _mod_rows(ref, i):
    b = i // (SEQ // ROW_TILE)
    return ref[pl.ds(DEC_BATCH + b, 1), :], ref[0:DEC_BATCH, :]


def _norm1_kernel(xp_ref, xs_ref, sh_ref, sc_ref, nw_ref, h_ref):
    i = pl.program_id(0)

    def run(x, sh, sc):
        h_ref[...] = (_rms(x, nw_ref[...]) * (1.0 + sc) + sh).astype(BF16)

    @pl.when(i < NP_ROW_TILES)
    def _():
        run(xp_ref[...], _mod_rows(sh_ref, i)[0], _mod_rows(sc_ref, i)[0])

    @pl.when(i == NP_ROW_TILES)
    def _():
        run(xs_ref[...], _mod_rows(sh_ref, i)[1], _mod_rows(sc_ref, i)[1])


def _prompt_tile(i):
    return (jnp.minimum(i, NP_ROW_TILES - 1), 0)


def _x_specs(split):
    if split:
        return [pl.BlockSpec((ROW_TILE, D_MODEL), _prompt_tile),
                pl.BlockSpec((ROW_TILE, D_MODEL), lambda i: (0, 0))]
    return [pl.BlockSpec((ROW_TILE, D_MODEL), _prompt_tile),
            pl.BlockSpec((ROW_TILE, D_MODEL), lambda i: (NP_ROW_TILES, 0))]


def _mod_spec(section):
    return pl.BlockSpec((N_COND, D_MODEL), lambda i: (0, section))


def _norm1(xp, xs, split, mod, norm_w, layer):
    return pl.pallas_call(
        _norm1_kernel,
        grid=(N_ROW_TILES,),
        in_specs=_x_specs(split) + [_mod_spec(0), _mod_spec(1),
                                    pl.BlockSpec((None, 1, D_MODEL), lambda i: (layer, 0, 0))],
        out_specs=pl.BlockSpec((ROW_TILE, D_MODEL), lambda i: (i, 0)),
        out_shape=jax.ShapeDtypeStruct((N_ROWS, D_MODEL), BF16),
        compiler_params=_cparams(("arbitrary",)),
        name="norm_mix",
    )(xp, xs, mod, mod, norm_w.reshape(DEPTH, 1, D_MODEL))


def _route(logits):
    lane_i = _iota(logits.shape, 1)
    lane = lane_i.astype(F32)
    neg = jnp.float32(-jnp.inf)
    big = jnp.float32(1 << 20)
    first = lambda hit: jnp.min(jnp.where(hit, lane, big), axis=-1, keepdims=True)
    gl = jnp.where(lane_i < N_GROUPS, logits, neg)
    gm = jnp.max(gl, axis=-1, keepdims=True)
    g_top = 1.0 / jnp.sum(jnp.exp(gl - gm), axis=-1, keepdims=True)
    g_idx = first(gl == gm)
    eid = lane_i - N_GROUPS
    group_of_lane = (eid >> 3).astype(F32)
    in_group = (eid >= 0) & (eid < N_EXPERTS) & (group_of_lane == g_idx)
    el = jnp.where(in_group, logits, neg)
    m1 = jnp.max(el, axis=-1, keepdims=True)
    i1 = first(el == m1)
    denom = jnp.sum(jnp.exp(el - m1), axis=-1, keepdims=True)
    el2 = jnp.where(lane == i1, neg, el)
    m2 = jnp.max(el2, axis=-1, keepdims=True)
    i2 = first(el2 == m2)
    p1 = 1.0 / denom
    p2 = jnp.exp(m2 - m1) / denom
    tot = p1 + p2
    out = jnp.where(lane_i == 0, i1 - N_GROUPS, 0.0)
    out = jnp.where(lane_i == 1, i2 - N_GROUPS, out)
    out = jnp.where(lane_i == 2, g_top * p1 / tot, out)
    out = jnp.where(lane_i == 3, g_top * p2 / tot, out)
    return out


def _mid_kernel(xp_ref, xs_ref, o_ref, gt_ref, sh_ref, sc_ref, nw_ref, wr_ref, br_ref,
                xn_ref, h_ref, rt_ref):
    i = pl.program_id(0)

    def run(x, gt, sh, sc):
        xn = x + gt * o_ref[...]
        xn_ref[...] = xn
        h = _rms(xn, nw_ref[...]) * (1.0 + sc) + sh
        h_ref[...] = h
        logits = jnp.dot(h, wr_ref[...], precision=lax.Precision.HIGHEST,
                         preferred_element_type=F32) + br_ref[...]
        rt_ref[...] = _route(logits)

    @pl.when(i < NP_ROW_TILES)
    def _():
        run(xp_ref[...], _mod_rows(gt_ref, i)[0], _mod_rows(sh_ref, i)[0], _mod_rows(sc_ref, i)[0])

    @pl.when(i == NP_ROW_TILES)
    def _():
        run(xs_ref[...], _mod_rows(gt_ref, i)[1], _mod_rows(sh_ref, i)[1], _mod_rows(sc_ref, i)[1])


def _mid(xp, xs, split, o, mod, norm_w, w_router, b_router, layer):
    row_spec = pl.BlockSpec((ROW_TILE, D_MODEL), lambda i: (i, 0))
    return pl.pallas_call(
        _mid_kernel,
        grid=(N_ROW_TILES,),
        in_specs=_x_specs(split) + [row_spec, _mod_spec(2), _mod_spec(3), _mod_spec(4),
                                    pl.BlockSpec((None, 1, D_MODEL), lambda i: (layer, 0, 0)),
                                    pl.BlockSpec((None, D_MODEL, LANES), lambda i: (layer, 0, 0)),
                                    pl.BlockSpec((None, 1, LANES), lambda i: (layer, 0, 0))],
        out_specs=[row_spec, row_spec, pl.BlockSpec((ROW_TILE, LANES), lambda i: (i, 0))],
        out_shape=[jax.ShapeDtypeStruct((N_ROWS, D_MODEL), F32),
                   jax.ShapeDtypeStruct((N_ROWS, D_MODEL), F32),
                   jax.ShapeDtypeStruct((N_ROWS, LANES), F32)],
        compiler_params=_cparams(("arbitrary",)),
        name="norm_ffn_router",
    )(xp, xs, o, mod, mod, mod, norm_w.reshape(DEPTH, 1, D_MODEL), w_router, b_router)


def _gather_rows(idx_ref, base, n, src_hbm, dst_ref, sem):
    def copy(r):
        return pltpu.make_async_copy(src_hbm.at[pl.ds(idx_ref[base + r], 1), :],
                                     dst_ref.at[pl.ds(r, 1), :], sem)

    def start(r, carry):
        copy(r).start()
        return carry

    def wait(r, carry):
        copy(r).wait()
        return carry

    lax.fori_loop(0, n, start, 0)
    lax.fori_loop(0, n, wait, 0)


def _combine_kernel(pos_ref, x_ref, rt_ref, gt_ref, y_hbm, nw_ref, sh_ref, sc_ref, *rest, final):
    if final:
        yp_ref, ys_ref, buf_ref, sem = rest
    else:
        xn_ref, h_ref, buf_ref, sem = rest
    i = pl.program_id(0)
    _gather_rows(pos_ref, i * 2 * ROW_TILE, 2 * ROW_TILE, y_hbm, buf_ref, sem)
    rt = rt_ref[...]
    moe = rt[:, 2:3] * buf_ref[0:ROW_TILE, :] + rt[:, 3:4] * buf_ref[ROW_TILE:2 * ROW_TILE, :]

    def run(gt, sh, sc, out_ref):
        xn = x_ref[...] + gt * moe
        if final:
            out_ref[...] = _rms(xn, nw_ref[...])
        else:
            xn_ref[...] = xn
            h_ref[...] = (_rms(xn, nw_ref[...]) * (1.0 + sc) + sh).astype(BF16)

    @pl.when(i < NP_ROW_TILES)
    def _():
        run(_mod_rows(gt_ref, i)[0], _mod_rows(sh_ref, i)[0], _mod_rows(sc_ref, i)[0],
            yp_ref if final else None)

    @pl.when(i == NP_ROW_TILES)
    def _():
        run(_mod_rows(gt_ref, i)[1], _mod_rows(sh_ref, i)[1], _mod_rows(sc_ref, i)[1],
            ys_ref if final else None)


def _combine(pos, x, route, mod, y_sorted, norm_w, norm_layer, mod_next, final):
    row_spec = pl.BlockSpec((ROW_TILE, D_MODEL), lambda i, p: (i, 0))
    mod_spec = lambda s: pl.BlockSpec((N_COND, D_MODEL), lambda i, p: (0, s))
    if final:
        out_specs = [pl.BlockSpec((ROW_TILE, D_MODEL), lambda i, p: _prompt_tile(i)),
                     pl.BlockSpec((ROW_TILE, D_MODEL), lambda i, p: (0, 0))]
        out_shape = [jax.ShapeDtypeStruct((NP_ROWS, D_MODEL), F32),
                     jax.ShapeDtypeStruct((DEC_BATCH, D_MODEL), F32)]
    else:
        out_specs = [row_spec, row_spec]
        out_shape = [jax.ShapeDtypeStruct((N_ROWS, D_MODEL), F32),
                     jax.ShapeDtypeStruct((N_ROWS, D_MODEL), BF16)]
    grid_spec = pltpu.PrefetchScalarGridSpec(
        num_scalar_prefetch=1,
        grid=(N_ROW_TILES,),
        in_specs=[row_spec, pl.BlockSpec((ROW_TILE, LANES), lambda i, p: (i, 0)), mod_spec(5),
                  pl.BlockSpec(memory_space=pl.ANY),
                  pl.BlockSpec((None, 1, D_MODEL), lambda i, p: (norm_layer, 0, 0)),
                  mod_spec(0), mod_spec(1)],
        out_specs=out_specs,
        scratch_shapes=[pltpu.VMEM((2 * ROW_TILE, D_MODEL), F32), pltpu.SemaphoreType.DMA(())],
    )
    return pl.pallas_call(
        functools.partial(_combine_kernel, final=final),
        grid_spec=grid_spec,
        out_shape=out_shape,
        compiler_params=_cparams(("arbitrary",)),
        name="moe_combine",
    )(pos, x, route, mod, y_sorted, norm_w, mod_next, mod_next)


def _expert_kernel(te_ref, nv_ref, tok_ref, h_hbm, wg_ref, wu_ref, wd_ref, y_ref, xg_ref, xb_ref, sem):
    i = pl.program_id(0)
    j = pl.program_id(1)
    valid = i < nv_ref[0]

    @pl.when(valid & (j == 0))
    def _():
        _gather_rows(tok_ref, i * MOE_TM, MOE_TM, h_hbm, xg_ref, sem)
        xb_ref[...] = xg_ref[...].astype(BF16)

    @pl.when(valid)
    def _():
        x = xb_ref[...]
        g = jnp.dot(x, wg_ref[...].astype(BF16), preferred_element_type=F32)
        u = jnp.dot(x, wu_ref[...].astype(BF16), preferred_element_type=F32)
        part = jnp.dot((_silu(g) * u).astype(BF16), wd_ref[...].astype(BF16),
                       preferred_element_type=F32)

        @pl.when(j == 0)
        def _():
            y_ref[...] = part

        @pl.when(j > 0)
        def _():
            y_ref[...] += part


def _experts(tile_e, n_valid, row_tok, h, w_gate, w_up, w_down, layer):
    n_f = D_EXPERT // MOE_FC

    def f_idx(i, j, nv):
        return jnp.where(i < nv[0], j, n_f - 1)

    grid_spec = pltpu.PrefetchScalarGridSpec(
        num_scalar_prefetch=3,
        grid=(MOE_TILES, n_f),
        in_specs=[pl.BlockSpec(memory_space=pl.ANY),
                  pl.BlockSpec((None, None, D_MODEL, MOE_FC),
                               lambda i, j, te, nv, tok: (layer, te[i], 0, f_idx(i, j, nv))),
                  pl.BlockSpec((None, None, D_MODEL, MOE_FC),
                               lambda i, j, te, nv, tok: (layer, te[i], 0, f_idx(i, j, nv))),
                  pl.BlockSpec((None, None, MOE_FC, D_MODEL),
                               lambda i, j, te, nv, tok: (layer, te[i], f_idx(i, j, nv), 0))],
        out_specs=pl.BlockSpec((MOE_TM, D_MODEL),
                               lambda i, j, te, nv, tok: (jnp.minimum(i, nv[0] - 1), 0)),
        scratch_shapes=[pltpu.VMEM((MOE_TM, D_MODEL), F32), pltpu.VMEM((MOE_TM, D_MODEL), BF16),
                        pltpu.SemaphoreType.DMA(())],
    )
    return pl.pallas_call(
        _expert_kernel,
        grid_spec=grid_spec,
        out_shape=jax.ShapeDtypeStruct((MOE_TILES * MOE_TM, D_MODEL), F32),
        compiler_params=_cparams(("arbitrary", "arbitrary")),
        name="moe_experts",
    )(tile_e, n_valid, row_tok, h, w_gate, w_up, w_down)


def _dispatch(route):
    n_assign = 2 * N_ROWS
    flat_e = route[:, 0:2].astype(jnp.int32).reshape(-1)
    order = jnp.argsort(flat_e, stable=True).astype(jnp.int32)
    se = flat_e[order]
    counts = jnp.bincount(flat_e, length=N_EXPERTS).astype(jnp.int32)
    start = jnp.cumsum(counts) - counts
    padded = (counts + MOE_TM - 1) // MOE_TM * MOE_TM
    pad_end = jnp.cumsum(padded)
    pad_start = pad_end - padded
    dest = pad_start[se] + jnp.arange(n_assign, dtype=jnp.int32) - start[se]
    row_tok = jnp.zeros((MOE_TILES * MOE_TM,), jnp.int32).at[dest].set(order // 2)
    pos = jnp.zeros((n_assign,), jnp.int32).at[order].set(dest)
    pos = pos.reshape(N_ROW_TILES, ROW_TILE, 2).transpose(0, 2, 1).reshape(-1)
    n_valid = (pad_end[-1] // MOE_TM).astype(jnp.int32)
    tile_start = jnp.arange(MOE_TILES, dtype=jnp.int32) * MOE_TM
    tile_e = jnp.minimum(jnp.searchsorted(pad_end, tile_start, side='right'), N_EXPERTS - 1)
    last_e = tile_e[jnp.maximum(n_valid - 1, 0)]
    tile_e = jnp.where(tile_start < pad_end[-1], tile_e, last_e).astype(jnp.int32)
    return tile_e, n_valid.reshape(1), row_tok, pos


def _rwkv_features(r, k, v, xwa, xg, prm):
    (w0, wup, a0, aup, gup, kkw, kaw, rkw) = prm
    lo = _iota(r.shape, 1) < A_HEAD
    w = w0 + _bdot(jnp.tanh(xwa), wup)
    w = -_softplus(-w) - 0.5
    lw = -jnp.exp(w)
    a = _sigmoid(a0 + _bdot(xwa, aup))
    g = _bdot(_sigmoid(xg), gup)
    kk = k * kkw
    kk = kk * lax.rsqrt(_half_sum(kk * kk, lo) + NORM_EPS)
    k2 = k * (1.0 + (a - 1.0) * kaw)
    bonus = _half_sum(r * k2 * rkw, lo) * v
    return lw, kk, k2, kk * a, g, bonus


def _rwkv_out(y, bonus, g, lnw, lnb):
    lo = _iota(y.shape, 1) < A_HEAD
    mu = _half_sum(y, lo) * (1.0 / A_HEAD)
    d = y - mu
    var = _half_sum(d * d, lo) * (1.0 / A_HEAD)
    yn = d * lax.rsqrt(var + A_LN_EPS) * lnw + lnb
    return ((yn + bonus) * g).astype(BF16)


def _rwkv_chunk_kernel(r_ref, k_ref, v_ref, xwa_ref, xg_ref, mr_ref, mk_ref, mv_ref, mwa_ref, mg_ref,
                       w0_ref, wup_ref, a0_ref, aup_ref, gup_ref, kkw_ref, kaw_ref, rkw_ref,
                       lnw_ref, lnb_ref, o_ref, s_out_ref, carry_ref, s_ref):
    c = pl.program_id(2)
    n = A_CHUNK

    @pl.when(c == 0)
    def _():
        carry_ref[...] = jnp.zeros_like(carry_ref)
        s_ref[...] = jnp.zeros_like(s_ref)

    row0 = _iota((n, LANES), 0) == 0

    def shifted(x_ref, mu_ref, slot):
        x = x_ref[...]
        prev = jnp.where(row0, carry_ref[slot, 0:1, :], pltpu.roll(x, 1, 0))
        carry_ref[slot, 0:1, :] = x[n - 1:n, :]
        return x + (prev - x) * mu_ref[...]

    r = shifted(r_ref, mr_ref, 0)
    k = shifted(k_ref, mk_ref, 1)
    v = shifted(v_ref, mv_ref, 2)
    xwa = shifted(xwa_ref, mwa_ref, 3)
    xg = shifted(xg_ref, mg_ref, 4)
    prm = (w0_ref[...], wup_ref[...], a0_ref[...], aup_ref[...], gup_ref[...], kkw_ref[...],
           kaw_ref[...], rkw_ref[...])
    lw, kk, k2, b, g, bonus = _rwkv_features(r, k, v, xwa, xg, prm)

    lo = _iota((n, LANES), 1) < A_HEAD
    cum = _cumsum_rows(lw, n)
    cum_prev = cum - lw
    cum_end = cum[n - 1:n, :]
    e_cum = jnp.exp(cum)
    e_inv = jnp.exp(-cum)
    e_end = jnp.exp(cum_end - cum)
    r_dec = r * e_cum
    kk_dec = kk * jnp.exp(cum_prev)
    k_inv = k2 * e_inv
    b_inv = b * e_inv
    stack = lambda x: jnp.concatenate([jnp.where(lo, x, 0.0), jnp.where(lo, 0.0, x)], axis=0)
    att = _bdot_nt(jnp.concatenate([stack(kk_dec), stack(r_dec)], axis=0),
                   jnp.concatenate([stack(b_inv), k_inv], axis=0))
    tr = _iota((2 * n, 2 * n), 0) & (n - 1)
    tc = _iota((2 * n, 2 * n), 1) & (n - 1)
    a_kkb = jnp.where(tr > tc, att[0:2 * n, 0:2 * n], 0.0)
    a_rb = jnp.where(tr >= tc, att[2 * n:4 * n, 0:2 * n], 0.0)
    tr = _iota((2 * n, n), 0) & (n - 1)
    tc = _iota((2 * n, n), 1)
    a_kkk = jnp.where(tr > tc, att[0:2 * n, 2 * n:3 * n], 0.0)
    a_rk = jnp.where(tr >= tc, att[2 * n:4 * n, 2 * n:3 * n], 0.0)

    s = s_ref[...]
    twice = lambda x: jnp.concatenate([x, x], axis=0)
    unstack = lambda x: jnp.where(lo, x[0:n, :], x[n:2 * n, :])
    inv = _unit_lower_inverse(a_kkb, 2 * n, n)
    u_st = _bdot(inv, twice(_bdot_nt(kk_dec, s)) + _bdot(a_kkk, v))
    y_st = twice(_bdot_nt(r_dec, s)) + _bdot(a_rk, v) - _bdot(a_rb, u_st)
    u = unstack(u_st)
    y = unstack(y_st)
    upd = _bdot_tn(jnp.concatenate([v, u], axis=0),
                   jnp.concatenate([k2 * e_end, -(b * e_end)], axis=0))
    same = (_iota((LANES, LANES), 0) < A_HEAD) == (_iota((LANES, LANES), 1) < A_HEAD)
    s_new = s * jnp.exp(cum_end) + jnp.where(same, upd, 0.0)
    s_ref[...] = s_new
    o_ref[...] = _rwkv_out(y, bonus, g, lnw_ref[...], lnb_ref[...])

    @pl.when(c == pl.num_programs(2) - 1)
    def _():
        s_out_ref[0] = s_new[0:A_HEAD, 0:A_HEAD]
        s_out_ref[1] = pltpu.roll(s_new, A_HEAD, 1)[A_HEAD:LANES, 0:A_HEAD]


def _rwkv_params(p):
    f = lambda name: p[name].reshape(1, A_WIDTH)
    wup = jnp.pad(p['a_w_up'][0], ((0, LANES - A_LORA), (0, 0)))
    aup = jnp.pad(p['a_a_up'][0], ((LANES - A_LORA, 0), (0, 0)))
    return (f('a_w0'), wup, f('a_a0'), aup, p['a_g_up'][0], f('a_k_k'), f('a_k_a'), f('a_r_k'),
            f('a_ln_w'), f('a_ln_b'))


def _rwkv_prompt(z, p):
    nc = SEQ // A_CHUNK
    hp_n = A_HEADS // 2
    zspec = lambda cb: pl.BlockSpec((A_CHUNK, LANES), lambda b, h, c: (b * nc + c, cb(h)))
    muspec = lambda cb: pl.BlockSpec((1, LANES), lambda b, h, c: (0, cb(h)))
    cols = [lambda h: h, lambda h: hp_n + h, lambda h: 2 * hp_n + h, lambda h: 3 * hp_n,
            lambda h: 3 * hp_n + 1]
    vec = lambda: pl.BlockSpec((1, LANES), lambda b, h, c: (0, h))
    mat = lambda: pl.BlockSpec((LANES, LANES), lambda b, h, c: (0, h))
    in_specs = ([zspec(cb) for cb in cols] + [muspec(cb) for cb in cols]
                + [vec(), mat(), vec(), mat(), mat(), vec(), vec(), vec(), vec(), vec()])
    mu = p['a_mu'].reshape(1, A_COLS)
    return pl.pallas_call(
        _rwkv_chunk_kernel,
        grid=(BATCH, hp_n, nc),
        in_specs=in_specs,
        out_specs=[pl.BlockSpec((A_CHUNK, LANES), lambda b, h, c: (b * nc + c, h)),
                   pl.BlockSpec((None, 2, A_HEAD, A_HEAD), lambda b, h, c: (b, h, 0, 0))],
        out_shape=[jax.ShapeDtypeStruct((N_ROWS, A_WIDTH), BF16),
                   jax.ShapeDtypeStruct((BATCH, A_HEADS, A_HEAD, A_HEAD), F32)],
        scratch_shapes=[pltpu.VMEM((5, 8, LANES), F32), pltpu.VMEM((LANES, LANES), F32)],
        compiler_params=_cparams(("arbitrary", "arbitrary", "arbitrary")),
        name="rwkv7_chunk",
    )(z, z, z, z, z, mu, mu, mu, mu, mu, *_rwkv_params(p))


def _rwkv_step_kernel(r_ref, k_ref, v_ref, xwa_ref, xg_ref, pr_ref, pk_ref, pv_ref, pwa_ref, pg_ref,
                      mr_ref, mk_ref, mv_ref, mwa_ref, mg_ref,
                      w0_ref, wup_ref, a0_ref, aup_ref, gup_ref, kkw_ref, kaw_ref, rkw_ref,
                      lnw_ref, lnb_ref, s_in_ref, o_alias_ref, o_ref, s_out_ref, q_ref, y_ref):
    del o_alias_ref
    nb = r_ref.shape[0]
    shifted = lambda x_ref, p_ref, mu_ref: x_ref[...] + (p_ref[...] - x_ref[...]) * mu_ref[...]
    r = shifted(r_ref, pr_ref, mr_ref)
    k = shifted(k_ref, pk_ref, mk_ref)
    v = shifted(v_ref, pv_ref, mv_ref)
    xwa = shifted(xwa_ref, pwa_ref, mwa_ref)
    xg = shifted(xg_ref, pg_ref, mg_ref)
    prm = (w0_ref[...], wup_ref[...], a0_ref[...], aup_ref[...], gup_ref[...], kkw_ref[...],
           kaw_ref[...], rkw_ref[...])
    lw, kk, k2, b, g, bonus = _rwkv_features(r, k, v, xwa, xg, prm)
    for slot, x in enumerate((kk, jnp.exp(lw), b, v, k2, r)):
        q_ref[0, slot] = x
        q_ref[1, slot] = pltpu.roll(x, A_HEAD, 1)
    y_ref[...] = jnp.zeros_like(y_ref)

    def body(i, carry):
        for hh in range(2):
            row = lambda slot: q_ref[hh, slot, pl.ds(i, 1), :][:, 0:A_HEAD]
            s = s_in_ref[i, hh]
            sa = jnp.sum(s * row(0), axis=1, keepdims=True)
            s = s * row(1) - sa * row(2) + _row_to_col(row(3), A_HEAD) * row(4)
            s_out_ref[i, hh] = s
            y_col = jnp.sum(s * row(5), axis=1, keepdims=True)
            y_ref[hh, pl.ds(i, 1), 0:A_HEAD] = _col_to_row(y_col, A_HEAD)
        return carry

    lax.fori_loop(0, nb, body, 0)
    lo = _iota((nb, LANES), 1) < A_HEAD
    y = jnp.where(lo, y_ref[0], pltpu.roll(y_ref[1], A_HEAD, 1))
    o_ref[...] = _rwkv_out(y, bonus, g, lnw_ref[...], lnb_ref[...])


def _rwkv_sample(z, shift, s0, p, o_all):
    hp_n = A_HEADS // 2
    rb = NP_ROWS // DEC_BATCH
    cols = [lambda h: h, lambda h: hp_n + h, lambda h: 2 * hp_n + h, lambda h: 3 * hp_n,
            lambda h: 3 * hp_n + 1]
    zspec = lambda cb: pl.BlockSpec((DEC_BATCH, LANES), lambda h: (rb, cb(h)))
    pspec = lambda cb: pl.BlockSpec((DEC_BATCH, LANES), lambda h: (0, cb(h)))
    muspec = lambda cb: pl.BlockSpec((1, LANES), lambda h: (0, cb(h)))
    vec = lambda: pl.BlockSpec((1, LANES), lambda h: (0, h))
    mat = lambda: pl.BlockSpec((LANES, LANES), lambda h: (0, h))
    sspec = pl.BlockSpec((DEC_BATCH, 2, A_HEAD, A_HEAD), lambda h: (0, h, 0, 0))
    in_specs = ([zspec(cb) for cb in cols] + [pspec(cb) for cb in cols] + [muspec(cb) for cb in cols]
                + [vec(), mat(), vec(), mat(), mat(), vec(), vec(), vec(), vec(), vec()]
                + [sspec, pl.BlockSpec(memory_space=pl.ANY)])
    mu = p['a_mu'].reshape(1, A_COLS)
    n_in = len(in_specs)
    return pl.pallas_call(
        _rwkv_step_kernel,
        grid=(hp_n,),
        in_specs=in_specs,
        out_specs=[pl.BlockSpec((DEC_BATCH, LANES), lambda h: (rb, h)), sspec],
        out_shape=[jax.ShapeDtypeStruct((N_ROWS, A_WIDTH), BF16),
                   jax.ShapeDtypeStruct((DEC_BATCH, A_HEADS, A_HEAD, A_HEAD), F32)],
        scratch_shapes=[pltpu.VMEM((2, 6, DEC_BATCH, LANES), F32), pltpu.VMEM((2, DEC_BATCH, LANES), F32)],
        input_output_aliases={n_in - 1: 0},
        compiler_params=_cparams(("arbitrary",)),
        name="rwkv7_step",
    )(z, z, z, z, z, shift, shift, shift, shift, shift, mu, mu, mu, mu, mu, *_rwkv_params(p), s0, o_all)


def _gdn_gates(gates, scal, h):
    lane = _iota(gates.shape, 1)
    a_log = scal[0:1, :]
    dt_bias = scal[1:2, :]
    g_all = -jnp.exp(a_log) * _softplus(gates + dt_bias)
    beta = jnp.sum(jnp.where(lane == h, _sigmoid(gates), 0.0), axis=-1, keepdims=True)
    g = jnp.sum(jnp.where(lane == B_HEADS + h, g_all, 0.0), axis=-1, keepdims=True)
    return beta, g


def _gdn_qkv(qc, kc, vc):
    q = _silu(qc)
    k = _silu(kc)
    q = q * lax.rsqrt(jnp.sum(q * q, axis=-1, keepdims=True) + NORM_EPS) * (B_HEAD ** -0.5)
    k = k * lax.rsqrt(jnp.sum(k * k, axis=-1, keepdims=True) + NORM_EPS)
    return q, k, _silu(vc)


def _gdn_out(o, zg, nw):
    return (_rms(o, nw) * _silu(zg)).astype(BF16)


def _gdn_chunk_kernel(q_ref, k_ref, v_ref, zg_ref, gt_ref, cwq_ref, cwk_ref, cwv_ref, scal_ref, nw_ref,
                      o_ref, s_out_ref, carry_ref, s_ref):
    h = pl.program_id(1)
    c = pl.program_id(2)
    n = B_CHUNK

    @pl.when(c == 0)
    def _():
        carry_ref[...] = jnp.zeros_like(carry_ref)
        s_ref[...] = jnp.zeros_like(s_ref)

    def conv(x_ref, w_ref, slot):
        x = x_ref[...]
        ext = jnp.concatenate([carry_ref[slot], x], axis=0)
        carry_ref[slot] = x[n - 8:n, :]
        y = ext[8:8 + n, :] * w_ref[CONV_W - 1:CONV_W, :]
        for j in range(CONV_W - 1):
            y = y + pltpu.roll(ext, CONV_W - 1 - j, 0)[8:8 + n, :] * w_ref[j:j + 1, :]
        return y

    q, k, v = _gdn_qkv(conv(q_ref, cwq_ref, 0), conv(k_ref, cwk_ref, 1), conv(v_ref, cwv_ref, 2))
    beta, g = _gdn_gates(gt_ref[...], scal_ref[...], h)
    cum = _cumsum_rows(jnp.broadcast_to(g, (n, LANES)), n)
    cum_end = cum[n - 1:n, :]
    lane0 = jnp.where(_iota((n, LANES), 1) == 0, 1.0, 0.0)
    cum_row = _fdot_nt(lane0, cum)
    tr = _iota((n, n), 0)
    tc = _iota((n, n), 1)
    dmat = jnp.where(tr >= tc, jnp.exp(jnp.where(tr >= tc, cum[:, 0:n] - cum_row, 0.0)), 0.0)
    gram = _bdot_nt(jnp.concatenate([k, q], axis=0), k)
    low = jnp.where(tr > tc, beta * dmat * gram[0:n, :], 0.0)
    inv = _unit_lower_inverse(low, n, n)
    e_cum = jnp.exp(cum)
    sol = _bdot(inv, jnp.concatenate([beta * v, beta * e_cum * k], axis=1))
    u_v = sol[:, 0:LANES]
    w_k = sol[:, LANES:2 * LANES]
    s = s_ref[...]
    u = u_v - _bdot_nt(w_k, s)
    o = _bdot_nt(q * e_cum, s) + _bdot(dmat * gram[n:2 * n, :], u)
    s_new = s * jnp.exp(cum_end[:, 0:1]) + _bdot_tn(u, k * jnp.exp(cum_end - cum))
    s_ref[...] = s_new
    o_ref[...] = _gdn_out(o, zg_ref[...], nw_ref[...])

    @pl.when(c == pl.num_programs(2) - 1)
    def _():
        s_out_ref[...] = s_new


def _gdn_scalars(p):
    scal = jnp.zeros((8, LANES), F32)
    scal = scal.at[0, B_HEADS:2 * B_HEADS].set(p['b_A_log'][0])
    return scal.at[1, B_HEADS:2 * B_HEADS].set(p['b_dt_bias'][0])


def _gdn_prompt(z, p):
    nc = SEQ // B_CHUNK
    cb0 = A_COLS // LANES
    zspec = lambda off: pl.BlockSpec((B_CHUNK, LANES), lambda b, h, c: (b * nc + c, cb0 + off + h))
    cwspec = lambda off: pl.BlockSpec((None, CONV_W, LANES), lambda b, h, c: (0, 0, off + h))
    in_specs = [zspec(0), zspec(B_HEADS), zspec(2 * B_HEADS), zspec(3 * B_HEADS),
                pl.BlockSpec((B_CHUNK, LANES), lambda b, h, c: (b * nc + c, cb0 + 4 * B_HEADS)),
                cwspec(0), cwspec(B_HEADS), cwspec(2 * B_HEADS),
                pl.BlockSpec((8, LANES), lambda b, h, c: (0, 0)),
                pl.BlockSpec((1, LANES), lambda b, h, c: (0, 0))]
    cw = p['b_conv_w']
    return pl.pallas_call(
        _gdn_chunk_kernel,
        grid=(BATCH, B_HEADS, nc),
        in_specs=in_specs,
        out_specs=[pl.BlockSpec((B_CHUNK, LANES), lambda b, h, c: (b * nc + c, h)),
                   pl.BlockSpec((None, None, B_HEAD, B_HEAD), lambda b, h, c: (b, h, 0, 0))],
        out_shape=[jax.ShapeDtypeStruct((N_ROWS, B_WIDTH), BF16),
                   jax.ShapeDtypeStruct((BATCH, B_HEADS, B_HEAD, B_HEAD), F32)],
        scratch_shapes=[pltpu.VMEM((3, 8, LANES), F32), pltpu.VMEM((B_HEAD, B_HEAD), F32)],
        compiler_params=_cparams(("arbitrary", "arbitrary", "arbitrary")),
        name="gdn_chunk",
    )(z, z, z, z, z, cw, cw, cw, _gdn_scalars(p), p['b_norm_w'])


def _gdn_step_kernel(q_ref, k_ref, v_ref, zg_ref, gt_ref, bq_ref, bk_ref, bv_ref, cwq_ref, cwk_ref,
                     cwv_ref, scal_ref, nw_ref, s_in_ref, o_alias_ref, o_ref, s_out_ref, x_ref, y_ref):
    del o_alias_ref
    h = pl.program_id(0)
    nb = q_ref.shape[0]

    def conv(x_ref_, buf_ref, w_ref):
        y = x_ref_[...] * w_ref[CONV_W - 1:CONV_W, :]
        for j in range(CONV_W - 1):
            y = y + buf_ref[:, j, :] * w_ref[j:j + 1, :]
        return y

    q, k, v = _gdn_qkv(conv(q_ref, bq_ref, cwq_ref), conv(k_ref, bk_ref, cwk_ref),
                       conv(v_ref, bv_ref, cwv_ref))
    beta, g = _gdn_gates(gt_ref[...], scal_ref[...], h)
    e_g = jnp.exp(g)
    x_ref[0] = k
    x_ref[1] = q
    x_ref[2] = beta * v
    x_ref[3] = jnp.broadcast_to(beta * e_g, (nb, LANES))
    x_ref[4] = jnp.broadcast_to(e_g, (nb, LANES))
    x_ref[5] = jnp.broadcast_to(jnp.sum(q * k, axis=-1, keepdims=True), (nb, LANES))

    def body(i, carry):
        row = lambda slot: x_ref[slot, pl.ds(i, 1), :]
        s = s_in_ref[i]
        k_row = row(0)
        e_col = row(4)[:, 0:1]
        s_k = jnp.sum(s * k_row, axis=1, keepdims=True)
        s_q = jnp.sum(s * row(1), axis=1, keepdims=True)
        u = _row_to_col(row(2), B_HEAD) - row(3)[:, 0:1] * s_k
        o_col = e_col * s_q + row(5)[:, 0:1] * u
        s_out_ref[i] = s * e_col + u * k_row
        y_ref[pl.ds(i, 1), :] = _col_to_row(o_col, B_HEAD)
        return carry

    lax.fori_loop(0, nb, body, 0)
    o_ref[...] = _gdn_out(y_ref[...], zg_ref[...], nw_ref[...])


GDN_STEP_BT = 64


def _gdn_sample(z, conv_buf, s0, p, o_alias):
    bt = GDN_STEP_BT
    rb = NP_ROWS // bt
    cb0 = A_COLS // LANES
    zspec = lambda off: pl.BlockSpec((bt, LANES), lambda h, b: (rb + b, cb0 + off + h))
    bspec = lambda off: pl.BlockSpec((bt, CONV_W - 1, LANES), lambda h, b: (b, 0, off + h))
    cwspec = lambda off: pl.BlockSpec((None, CONV_W, LANES), lambda h, b: (0, 0, off + h))
    sspec = pl.BlockSpec((bt, None, B_HEAD, B_HEAD), lambda h, b: (b, h, 0, 0))
    in_specs = [zspec(0), zspec(B_HEADS), zspec(2 * B_HEADS), zspec(3 * B_HEADS),
                pl.BlockSpec((bt, LANES), lambda h, b: (rb + b, cb0 + 4 * B_HEADS)),
                bspec(0), bspec(B_HEADS), bspec(2 * B_HEADS),
                cwspec(0), cwspec(B_HEADS), cwspec(2 * B_HEADS),
                pl.BlockSpec((8, LANES), lambda h, b: (0, 0)),
                pl.BlockSpec((1, LANES), lambda h, b: (0, 0)),
                sspec, pl.BlockSpec(memory_space=pl.ANY)]
    cw = p['b_conv_w']
    return pl.pallas_call(
        _gdn_step_kernel,
        grid=(B_HEADS, DEC_BATCH // bt),
        in_specs=in_specs,
        out_specs=[pl.BlockSpec((bt, LANES), lambda h, b: (rb + b, h)), sspec],
        out_shape=[jax.ShapeDtypeStruct((N_ROWS, B_WIDTH), BF16),
                   jax.ShapeDtypeStruct((DEC_BATCH, B_HEADS, B_HEAD, B_HEAD), F32)],
        scratch_shapes=[pltpu.VMEM((6, bt, LANES), F32), pltpu.VMEM((bt, LANES), F32)],
        input_output_aliases={14: 0},
        compiler_params=_cparams(("arbitrary", "arbitrary")),
        name="gdn_step",
    )(z, z, z, z, z, conv_buf, conv_buf, conv_buf, cw, cw, cw, _gdn_scalars(p), p['b_norm_w'], s0, o_alias)


def _rglru_gates(xc, gate, wr, br, wi, bi, lam):
    r = _sigmoid(_bdot(xc, wr) + br)
    i = _sigmoid(_bdot(xc, wi) + bi)
    log_a = -RGLRU_C * r * _softplus(-lam)
    a = jnp.exp(log_a)
    b = jnp.sqrt(1.0 - jnp.exp(2.0 * log_a)) * i * xc
    return a, b, _gelu_tanh(gate)


def _rglru_chunk_kernel(x_ref, g_ref, cw_ref, cb_ref, wr_ref, br_ref, wi_ref, bi_ref, lam_ref,
                        o_ref, h_out_ref, carry_ref, h_ref):
    c = pl.program_id(2)
    n = C_TILE

    @pl.when(c == 0)
    def _():
        carry_ref[...] = jnp.zeros_like(carry_ref)
        h_ref[...] = jnp.zeros_like(h_ref)

    x = x_ref[...]
    ext = jnp.concatenate([carry_ref[...], x], axis=0)
    carry_ref[...] = x[n - 8:n, :]
    xc = ext[8:8 + n, :] * cw_ref[CONV_W - 1:CONV_W, :] + cb_ref[...]
    for j in range(CONV_W - 1):
        xc = xc + pltpu.roll(ext, CONV_W - 1 - j, 0)[8:8 + n, :] * cw_ref[j:j + 1, :]
    a, b, gate = _rglru_gates(xc, g_ref[...], wr_ref[...], br_ref[...], wi_ref[...], bi_ref[...],
                              lam_ref[...])
    row = _iota((n, LANES), 0)
    s = 1
    while s < n:
        keep = row >= s
        a_sh = jnp.where(keep, pltpu.roll(a, s, 0), 1.0)
        b_sh = jnp.where(keep, pltpu.roll(b, s, 0), 0.0)
        b = a * b_sh + b
        a = a * a_sh
        s *= 2
    hs = a * h_ref[0:1, :] + b
    h_ref[0:1, :] = hs[n - 1:n, :]
    o_ref[...] = (hs * gate).astype(BF16)

    @pl.when(c == pl.num_programs(2) - 1)
    def _():
        h_out_ref[...] = hs[n - 1:n, :]


def _rglru_prompt(z, p):
    nc = SEQ // C_TILE
    vec = lambda: pl.BlockSpec((1, LANES), lambda b, j, c: (0, j))
    blk = lambda: pl.BlockSpec((None, None, LANES, LANES), lambda b, j, c: (0, j, 0, 0))
    in_specs = [pl.BlockSpec((C_TILE, LANES), lambda b, j, c: (b * nc + c, j)),
                pl.BlockSpec((C_TILE, LANES), lambda b, j, c: (b * nc + c, C_BLOCKS + j)),
                pl.BlockSpec((None, CONV_W, LANES), lambda b, j, c: (0, 0, j)),
                vec(), blk(), vec(), blk(), vec(), vec()]
    return pl.pallas_call(
        _rglru_chunk_kernel,
        grid=(BATCH, C_BLOCKS, nc),
        in_specs=in_specs,
        out_specs=[pl.BlockSpec((C_TILE, LANES), lambda b, j, c: (b * nc + c, j)),
                   pl.BlockSpec((None, 1, LANES), lambda b, j, c: (b, 0, j))],
        out_shape=[jax.ShapeDtypeStruct((N_ROWS, C_WIDTH), BF16),
                   jax.ShapeDtypeStruct((BATCH, 1, C_WIDTH), F32)],
        scratch_shapes=[pltpu.VMEM((8, LANES), F32), pltpu.VMEM((8, LANES), F32)],
        compiler_params=_cparams(("arbitrary", "arbitrary", "arbitrary")),
        name="rglru_chunk",
    )(z, z, p['c_conv_w'], p['c_conv_b'], p['c_w_r'], p['c_b_r'], p['c_w_i'], p['c_b_i'], p['c_lambda'])


def _rglru_step_kernel(x_ref, g_ref, buf_ref, h0_ref, cw_ref, cb_ref, wr_ref, br_ref, wi_ref, bi_ref,
                       lam_ref, o_alias_ref, o_ref, h_out_ref):
    del o_alias_ref
    xc = x_ref[...] * cw_ref[CONV_W - 1:CONV_W, :] + cb_ref[...]
    for j in range(CONV_W - 1):
        xc = xc + buf_ref[:, j, :] * cw_ref[j:j + 1, :]
    a, b, gate = _rglru_gates(xc, g_ref[...], wr_ref[...], br_ref[...], wi_ref[...], bi_ref[...],
                              lam_ref[...])
    hs = a * h0_ref[...] + b
    h_out_ref[...] = hs
    o_ref[...] = (hs * gate).astype(BF16)


def _rglru_sample(z, conv_buf, h0, p, o_alias):
    rb = NP_ROWS // DEC_BATCH
    vec = lambda: pl.BlockSpec((1, LANES), lambda j: (0, j))
    blk = lambda: pl.BlockSpec((None, None, LANES, LANES), lambda j: (0, j, 0, 0))
    in_specs = [pl.BlockSpec((DEC_BATCH, LANES), lambda j: (rb, j)),
                pl.BlockSpec((DEC_BATCH, LANES), lambda j: (rb, C_BLOCKS + j)),
                pl.BlockSpec((DEC_BATCH, CONV_W - 1, LANES), lambda j: (0, 0, j)),
                pl.BlockSpec((DEC_BATCH, LANES), lambda j: (0, j)),
                pl.BlockSpec((None, CONV_W, LANES), lambda j: (0, 0, j)),
                vec(), blk(), vec(), blk(), vec(), vec(), pl.BlockSpec(memory_space=pl.ANY)]
    return pl.pallas_call(
        _rglru_step_kernel,
        grid=(C_BLOCKS,),
        in_specs=in_specs,
        out_specs=[pl.BlockSpec((DEC_BATCH, LANES), lambda j: (rb, j)),
                   pl.BlockSpec((DEC_BATCH, LANES), lambda j: (0, j))],
        out_shape=[jax.ShapeDtypeStruct((N_ROWS, C_WIDTH), BF16),
                   jax.ShapeDtypeStruct((DEC_BATCH, C_WIDTH), F32)],
        input_output_aliases={11: 0},
        compiler_params=_cparams(("arbitrary",)),
        name="rglru_step",
    )(z, z, conv_buf, h0, p['c_conv_w'], p['c_conv_b'], p['c_w_r'], p['c_b_r'], p['c_w_i'], p['c_b_i'],
      p['c_lambda'], o_alias)


def _hgrn_gates(f, lb_ref, layer):
    raw = lb_ref[...]
    e = jnp.exp(raw - jnp.max(raw, axis=0, keepdims=True))
    sm = e / jnp.sum(e, axis=0, keepdims=True)
    lb = jnp.sum(sm[0:layer + 1, :], axis=0, keepdims=True) - sm[0:1, :]
    return lb + (1.0 - lb) * _sigmoid(f)


def _hgrn_out(o, og, nw):
    return (_rms(o, nw) * _silu(og)).astype(BF16)


def _hgrn_chunk_kernel(q_ref, f_ref, i_ref, og_ref, lb_ref, nw_ref, o_ref, s_out_ref, st_ref, *, layer):
    c = pl.program_id(2)
    n = D_TILE
    m = D_SUB

    @pl.when(c == 0)
    def _():
        st_ref[...] = jnp.zeros_like(st_ref)

    fg = _hgrn_gates(f_ref[...], lb_ref, layer)
    q = _silu(q_ref[...])
    k = 1.0 - fg
    v = i_ref[...]
    cum = _cumsum_rows(jnp.log(fg), m)
    q_dec = q * jnp.exp(cum)
    tr = _iota((m, LANES), 0)
    st = st_ref[...]
    outs = []
    for sub in range(n // m):
        sl = slice(sub * m, (sub + 1) * m)
        qs, ks, vs, cs = q[sl], k[sl], v[sl], cum[sl]
        o = _bdot_nt(q_dec[sl], st)
        for j in range(m):
            w = jnp.exp(jnp.minimum(cs - cs[j:j + 1, :], 0.0))
            att = jnp.sum(qs * w * ks[j:j + 1, :], axis=-1, keepdims=True)
            o = o + jnp.where(tr >= j, att, 0.0) * vs[j:j + 1, :]
        outs.append(o)
        c_end = cs[m - 1:m, :]
        st = st * jnp.exp(c_end) + _bdot_tn(vs, ks * jnp.exp(c_end - cs))
    st_ref[...] = st
    o_ref[...] = _hgrn_out(jnp.concatenate(outs, axis=0), og_ref[...], nw_ref[...])

    @pl.when(c == pl.num_programs(2) - 1)
    def _():
        s_out_ref[...] = st.T


def _hgrn_prompt(z, p, layer):
    nc = SEQ // D_TILE
    cb0 = C_COLS // LANES
    zspec = lambda off: pl.BlockSpec((D_TILE, LANES), lambda b, h, c: (b * nc + c, cb0 + off + h))
    in_specs = [zspec(0), zspec(D_HEADS), zspec(2 * D_HEADS), zspec(3 * D_HEADS),
                pl.BlockSpec((DEPTH, LANES), lambda b, h, c: (0, h)),
                pl.BlockSpec((1, LANES), lambda b, h, c: (0, 0))]
    return pl.pallas_call(
        functools.partial(_hgrn_chunk_kernel, layer=layer),
        grid=(BATCH, D_HEADS, nc),
        in_specs=in_specs,
        out_specs=[pl.BlockSpec((D_TILE, LANES), lambda b, h, c: (b * nc + c, h)),
                   pl.BlockSpec((None, None, LANES, LANES), lambda b, h, c: (b, h, 0, 0))],
        out_shape=[jax.ShapeDtypeStruct((N_ROWS, D_WIDTH), BF16),
                   jax.ShapeDtypeStruct((BATCH, D_HEADS, LANES, LANES), F32)],
        scratch_shapes=[pltpu.VMEM((LANES, LANES), F32)],
        compiler_params=_cparams(("arbitrary", "arbitrary", "arbitrary")),
        name="hgrn2_chunk",
    )(z, z, z, z, p['d_lb'], p['d_norm_w'])


def _hgrn_step_kernel(q_ref, f_ref, i_ref, og_ref, lb_ref, nw_ref, s_in_ref, o_alias_ref, o_ref,
                      s_out_ref, x_ref, y_ref, *, layer):
    del o_alias_ref
    nb = q_ref.shape[0]
    x_ref[0] = _hgrn_gates(f_ref[...], lb_ref, layer)
    x_ref[1] = _silu(q_ref[...])

    def body(i, carry):
        fg = _row_to_col(x_ref[0, pl.ds(i, 1), :], LANES)
        qc = _row_to_col(x_ref[1, pl.ds(i, 1), :], LANES)
        s = s_in_ref[i] * fg + (1.0 - fg) * i_ref[pl.ds(i, 1), :]
        s_out_ref[i] = s
        y_ref[pl.ds(i, 1), :] = jnp.sum(s * qc, axis=0, keepdims=True)
        return carry

    lax.fori_loop(0, nb, body, 0)
    o_ref[...] = _hgrn_out(y_ref[...], og_ref[...], nw_ref[...])


HGRN_STEP_BT = 64


def _hgrn_sample(z, s0, p, layer, o_alias):
    bt = HGRN_STEP_BT
    rb = NP_ROWS // bt
    cb0 = C_COLS // LANES
    zspec = lambda off: pl.BlockSpec((bt, LANES), lambda h, b: (rb + b, cb0 + off + h))
    sspec = pl.BlockSpec((bt, None, LANES, LANES), lambda h, b: (b, h, 0, 0))
    in_specs = [zspec(0), zspec(D_HEADS), zspec(2 * D_HEADS), zspec(3 * D_HEADS),
                pl.BlockSpec((DEPTH, LANES), lambda h, b: (0, h)),
                pl.BlockSpec((1, LANES), lambda h, b: (0, 0)),
                sspec, pl.BlockSpec(memory_space=pl.ANY)]
    return pl.pallas_call(
        functools.partial(_hgrn_step_kernel, layer=layer),
        grid=(D_HEADS, DEC_BATCH // bt),
        in_specs=in_specs,
        out_specs=[pl.BlockSpec((bt, LANES), lambda h, b: (rb + b, h)), sspec],
        out_shape=[jax.ShapeDtypeStruct((N_ROWS, D_WIDTH), BF16),
                   jax.ShapeDtypeStruct((DEC_BATCH, D_HEADS, LANES, LANES), F32)],
        scratch_shapes=[pltpu.VMEM((2, bt, LANES), F32), pltpu.VMEM((bt, LANES), F32)],
        input_output_aliases={7: 0},
        compiler_params=_cparams(("arbitrary", "arbitrary")),
        name="hgrn2_step",
    )(z, z, z, z, p['d_lb'], p['d_norm_w'], s0, o_alias)


def _last_rows(z, n_last, c0, c1):
    return jnp.stack([z[SEQ - n_last + j:NP_ROWS:SEQ, c0:c1] for j in range(n_last)], axis=1)


def kernel(x_prompt, x_sample, c_prompt, c_sample, state_a_shift, state_a_wkv, state_b_conv, state_b_delta,
           state_c_conv, state_c_h, state_d_s, w_mod, b_mod, norm_mix, norm_ffn, final_norm, w_in_even,
           w_out_even, a_mu, a_w0, a_w_up, a_a0, a_a_up, a_g_up, a_k_k, a_k_a, a_r_k, a_ln_w, a_ln_b,
           b_conv_w, b_A_log, b_dt_bias, b_norm_w, w_in_odd, w_out_odd, c_conv_w, c_conv_b, c_w_r, c_b_r,
           c_w_i, c_b_i, c_lambda, d_lb, d_norm_w, w_router_group, b_router_group, w_router_expert,
           b_router_expert, w_expert_gate, w_expert_up, w_expert_down):
    p = dict(a_mu=a_mu, a_w0=a_w0, a_w_up=a_w_up, a_a0=a_a0, a_a_up=a_a_up, a_g_up=a_g_up, a_k_k=a_k_k,
             a_k_a=a_k_a, a_r_k=a_r_k, a_ln_w=a_ln_w, a_ln_b=a_ln_b, b_conv_w=b_conv_w, b_A_log=b_A_log,
             b_dt_bias=b_dt_bias, b_norm_w=b_norm_w, c_conv_w=c_conv_w, c_conv_b=c_conv_b, c_w_r=c_w_r,
             c_b_r=c_b_r, c_w_i=c_w_i, c_b_i=c_b_i, c_lambda=c_lambda, d_lb=d_lb, d_norm_w=d_norm_w)
    cond = jnp.concatenate([c_sample, c_prompt], axis=0)
    w_router = jnp.pad(jnp.concatenate([w_router_group, w_router_expert], axis=-1),
                       ((0, 0), (0, 0), (0, LANES - N_GROUPS - N_EXPERTS)))
    b_router = jnp.pad(jnp.concatenate([b_router_group, b_router_expert], axis=-1),
                       ((0, 0), (0, LANES - N_GROUPS - N_EXPERTS))).reshape(DEPTH, 1, LANES)
    mods = [_matmul([cond], w_mod, layer, bias=b_mod, silu_in=True, tm=N_COND, name="adaln_mod")
            for layer in range(DEPTH)]

    xp = x_prompt.reshape(NP_ROWS, D_MODEL)
    xs = x_sample.reshape(DEC_BATCH, D_MODEL)
    split = True
    h = _norm1(xp, xs, True, mods[0], norm_mix, 0)
    new_p, new_s = {}, {}
    for layer in range(DEPTH):
        mod = mods[layer]
        if layer % 2 == 0:
            z = _matmul([h], w_in_even, 0, name="w_in_even")
            oa, new_p['a_wkv'] = _rwkv_prompt(z, p)
            oa, new_s['a_wkv'] = _rwkv_sample(z, state_a_shift[0], state_a_wkv[0], p, oa)
            ob, new_p['b_delta'] = _gdn_prompt(z, p)
            ob, new_s['b_delta'] = _gdn_sample(z, state_b_conv[0], state_b_delta[0], p, ob)
            o = _matmul([oa, ob], w_out_even, 0, name="w_out_even")
            new_p['a_shift'] = _last_rows(z, 1, 0, A_COLS)[:, 0]
            new_s['a_shift'] = z[NP_ROWS:, :A_COLS]
            new_p['b_conv'] = _last_rows(z, CONV_W - 1, A_COLS, A_COLS + 3 * B_WIDTH)
            new_s['b_conv'] = jnp.concatenate(
                [state_b_conv[0][:, 1:], z[NP_ROWS:, None, A_COLS:A_COLS + 3 * B_WIDTH]], axis=1)
        else:
            z = _matmul([h], w_in_odd, 0, name="w_in_odd")
            oc, hp = _rglru_prompt(z, p)
            new_p['c_h'] = hp.reshape(BATCH, C_WIDTH)
            oc, new_s['c_h'] = _rglru_sample(z, state_c_conv[0], state_c_h[0], p, oc)
            od, new_p['d_s'] = _hgrn_prompt(z, p, layer)
            od, new_s['d_s'] = _hgrn_sample(z, state_d_s[0], p, layer, od)
            o = _matmul([oc, od], w_out_odd, 0, name="w_out_odd")
            new_p['c_conv'] = _last_rows(z, CONV_W - 1, 0, C_WIDTH)
            new_s['c_conv'] = jnp.concatenate([state_c_conv[0][:, 1:], z[NP_ROWS:, None, :C_WIDTH]], axis=1)
        x_mid, h_ffn, route = _mid(xp, xs, split, o, mod, norm_ffn, w_router, b_router, layer)
        tile_e, n_valid, row_tok, pos = _dispatch(route)
        y_sorted = _experts(tile_e, n_valid, row_tok, h_ffn, w_expert_gate, w_expert_up, w_expert_down, layer)
        if layer + 1 < DEPTH:
            x_all, h = _combine(pos, x_mid, route, mod, y_sorted, norm_mix.reshape(DEPTH, 1, D_MODEL),
                                layer + 1, mods[layer + 1], final=False)
            xp = xs = x_all
            split = False
        else:
            y_p, y_s = _combine(pos, x_mid, route, mod, y_sorted, final_norm.reshape(1, 1, D_MODEL), 0,
                                mod, final=True)
    st = lambda d, key: d[key][None]
    return (y_p.reshape(BATCH, SEQ, D_MODEL), y_s.reshape(DEC_BATCH, 1, D_MODEL),
            st(new_p, 'a_shift'), st(new_p, 'a_wkv'), st(new_p, 'b_conv'), st(new_p, 'b_delta'),
            st(new_p, 'c_conv'), st(new_p, 'c_h'), st(new_p, 'd_s'),
            st(new_s, 'a_shift'), st(new_s, 'a_wkv'), st(new_s, 'b_conv'), st(new_s, 'b_delta'),
            st(new_s, 'c_conv'), st(new_s, 'c_h'), st(new_s, 'd_s'))
```

```python
import functools

import jax
import jax.numpy as jnp
from jax import lax
from jax.experimental import pallas as pl
from jax.experimental.pallas import tpu as pltpu

F32 = jnp.float32
BF16 = jnp.bfloat16

D_MODEL = 4096
BATCH = 4
SEQ = 2048
DEPTH = 2
DEC_BATCH = 128
NP_ROWS = BATCH * SEQ
N_ROWS = NP_ROWS + DEC_BATCH
N_COND = DEC_BATCH + BATCH

A_WIDTH = 2048
A_HEAD = 64
A_HEADS = 32
A_LORA = 64
A_COLS = 3 * A_WIDTH + 64 + 64 + 128
A_LN_EPS = 64e-5
B_WIDTH = 2048
B_HEAD = 128
B_HEADS = 16
B_COLS = 4 * B_WIDTH + 2 * B_HEADS
CONV_W = 4
C_WIDTH = 2048
C_BLOCKS = 16
C_COLS = 2 * C_WIDTH
RGLRU_C = 8.0
D_WIDTH = 2048
D_HEADS = 16
D_COLS = 4 * D_WIDTH
EVEN_COLS = A_COLS + B_COLS
ODD_COLS = C_COLS + D_COLS
N_GROUPS = 8
EXPERTS_PER_GROUP = 8
N_EXPERTS = 64
D_EXPERT = 1024
NORM_EPS = 1e-6

LANES = 128
ROW_TILE = 128
N_ROW_TILES = N_ROWS // ROW_TILE
NP_ROW_TILES = NP_ROWS // ROW_TILE
MM_TM = 1664
MM_TN = 512
A_CHUNK = 64
B_CHUNK = 64
RWKV_G = 16
GDN_G = 16
HGRN_G = 16
RGLRU_G = 8
D_SUB = 16
D_TILE = 128
C_TILE = 256
MOE_TM = 512
MOE_FC = 4
MOE_TILES = (2 * N_ROWS + MOE_TM - 1) // MOE_TM + N_EXPERTS
VMEM_LIMIT = 56 * 1024 * 1024
MOE_VMEM_LIMIT = 60 * 1024 * 1024


def _cparams(sem):
    return pltpu.CompilerParams(dimension_semantics=sem, vmem_limit_bytes=VMEM_LIMIT)


def _iota(shape, dim):
    return lax.broadcasted_iota(jnp.int32, shape, dim)


def _bdot(a, b):
    return jnp.dot(a.astype(BF16), b.astype(BF16), preferred_element_type=F32)


def _bdot_nt(a, b):
    return lax.dot_general(a.astype(BF16), b.astype(BF16), (((1,), (1,)), ((), ())),
                           preferred_element_type=F32)


def _bdot_tn(a, b):
    return lax.dot_general(a.astype(BF16), b.astype(BF16), (((0,), (0,)), ((), ())),
                           preferred_element_type=F32)


def _fdot_nt(a, b):
    return lax.dot_general(a, b, (((1,), (1,)), ((), ())), precision=lax.Precision.HIGHEST,
                           preferred_element_type=F32)


def _dot_split3(a, b):
    a_hi = a.astype(BF16)
    b_hi = b.astype(BF16)
    a_lo = (a - a_hi.astype(F32)).astype(BF16)
    b_lo = (b - b_hi.astype(F32)).astype(BF16)
    dot = lambda x, y: jnp.dot(x, y, preferred_element_type=F32)
    return dot(a_hi, b_hi) + (dot(a_lo, b_hi) + dot(a_hi, b_lo))


def _pack_bf16_pairs(x):
    n = x.shape[1] // 2
    bits = lax.bitcast_convert_type(x.astype(BF16).astype(F32), jnp.int32)
    return lax.shift_right_logical(bits[:, :n], jnp.int32(16)) | (bits[:, n:] & jnp.int32(-65536))


def _unpack_bf16_pairs(w):
    lo = lax.bitcast_convert_type(lax.shift_left(w, jnp.int32(16)), F32)
    hi = lax.bitcast_convert_type(w & jnp.int32(-65536), F32)
    return lo.astype(BF16), hi.astype(BF16)


def _sigmoid(x):
    return 1.0 / (1.0 + jnp.exp(-x))


def _silu(x):
    return x * _sigmoid(x)


def _softplus(x):
    return jnp.maximum(x, 0.0) + jnp.log(1.0 + jnp.exp(-jnp.abs(x)))


def _gelu_tanh(x):
    return 0.5 * x * (1.0 + jnp.tanh(0.7978845608028654 * (x + 0.044715 * x * x * x)))


def _cumsum_rows(x, seg):
    pos = _iota(x.shape, 0) & (seg - 1)
    s = 1
    while s < seg:
        x = x + jnp.where(pos >= s, pltpu.roll(x, s, 0), 0.0)
        s *= 2
    return x


def _unit_lower_inverses(lows, n, top):
    row = _iota((n, n), 0)
    col = _iota((n, n), 1)
    def merge_mask(s):
        sh = s.bit_length() - 1
        return (((row >> (sh + 1)) == (col >> (sh + 1))) & (((row >> sh) & 1) == 1)
                & (((col >> sh) & 1) == 0))

    eye = jnp.where(row == col, 1.0, 0.0)
    xs = [eye - jnp.where(merge_mask(1), low, 0.0) for low in lows]
    s = 2
    while s < top:
        sel = merge_mask(s)
        ts = [_bdot(x, jnp.where(sel, low, 0.0)) for x, low in zip(xs, lows)]
        xs = [x - _bdot(t, x) for x, t in zip(xs, ts)]
        s *= 2
    return xs


def _half_sum(x, lo):
    s_lo = jnp.sum(jnp.where(lo, x, 0.0), axis=-1, keepdims=True)
    s_hi = jnp.sum(jnp.where(lo, 0.0, x), axis=-1, keepdims=True)
    return jnp.where(lo, s_lo, s_hi)


def _row_to_col(row, n):
    eye = _iota((n, n), 0) == _iota((n, n), 1)
    return jnp.sum(jnp.where(eye, jnp.broadcast_to(row, (n, n)), 0.0), axis=1, keepdims=True)


def _col_to_row(col, n):
    eye = _iota((n, n), 0) == _iota((n, n), 1)
    return jnp.sum(jnp.where(eye, jnp.broadcast_to(col, (n, n)), 0.0), axis=0, keepdims=True)


def _mm_kernel(*refs, k_sizes, silu_in, has_bias, w_transposed):
    n_parts = len(k_sizes)
    x_refs = refs[:n_parts]
    w_ref = refs[n_parts]
    b_ref = refs[n_parts + 1] if has_bias else None
    o_ref = refs[n_parts + 1 + int(has_bias)]
    wb_ref = refs[n_parts + 2 + int(has_bias)]

    @pl.when(pl.program_id(1) == 0)
    def _():
        wb_ref[...] = w_ref[...].astype(BF16)

    acc = None
    off = 0
    for p in range(n_parts):
        x = x_refs[p][...]
        if silu_in:
            x = _silu(x.astype(F32))
        if w_transposed:
            part = lax.dot_general(x.astype(BF16), wb_ref[:, off:off + k_sizes[p]],
                                   (((1,), (1,)), ((), ())), preferred_element_type=F32)
        else:
            part = jnp.dot(x.astype(BF16), wb_ref[off:off + k_sizes[p], :], preferred_element_type=F32)
        acc = part if acc is None else acc + part
        off += k_sizes[p]
    if has_bias:
        acc = acc + b_ref[...]
    o_ref[...] = acc


def _matmul(parts, w, layer, bias=None, silu_in=False, w_transposed=False, tm=MM_TM, tn=MM_TN,
            name="proj"):
    m = parts[0].shape[0]
    k_sizes = tuple(p.shape[1] for p in parts)
    k_total = sum(k_sizes)
    n = w.shape[1] if w_transposed else w.shape[2]
    assert w.shape[2 if w_transposed else 1] == k_total and m % tm == 0
    grid = (pl.cdiv(n, tn), m // tm)
    in_specs = [pl.BlockSpec((tm, k), lambda j, i: (i, 0)) for k in k_sizes]
    if w_transposed:
        in_specs.append(pl.BlockSpec((None, tn, k_total), lambda j, i: (layer, j, 0)))
    else:
        in_specs.append(pl.BlockSpec((None, k_total, tn), lambda j, i: (layer, 0, j)))
    args = list(parts) + [w]
    if bias is not None:
        in_specs.append(pl.BlockSpec((None, 1, tn), lambda j, i: (layer, 0, j)))
        args.append(bias.reshape(bias.shape[0], 1, n))
    return pl.pallas_call(
        functools.partial(_mm_kernel, k_sizes=k_sizes, silu_in=silu_in, has_bias=bias is not None,
                          w_transposed=w_transposed),
        grid=grid,
        in_specs=in_specs,
        out_specs=pl.BlockSpec((tm, tn), lambda j, i: (i, j)),
        out_shape=jax.ShapeDtypeStruct((m, n), F32),
        scratch_shapes=[pltpu.VMEM((tn, k_total) if w_transposed else (k_total, tn), BF16)],
        compiler_params=_cparams(("arbitrary", "arbitrary")),
        name=name,
    )(*args)


def _tail_kernel(z_ref, o_ref):
    o_ref[...] = z_ref[...]


def _prompt_tails(z):
    cols = z.shape[1]
    return pl.pallas_call(
        _tail_kernel,
        grid=(BATCH,),
        in_specs=[pl.BlockSpec((8, cols), lambda b: ((b + 1) * (SEQ // 8) - 1, 0))],
        out_specs=pl.BlockSpec((None, 8, cols), lambda b: (b, 0, 0)),
        out_shape=jax.ShapeDtypeStruct((BATCH, 8, cols), F32),
        compiler_params=_cparams(("arbitrary",)),
        name="prompt_tails",
    )(z)


def _rms(x, w):
    return x * lax.rsqrt(jnp.mean(x * x, axis=-1, keepdims=True) + NORM_EPS) * w


def _mod_rows(ref, i):
    b = i // (SEQ // ROW_TILE)
    return ref[pl.ds(DEC_BATCH + b, 1), :], ref[0:DEC_BATCH, :]


def _norm1_kernel(xp_ref, xs_ref, sh_ref, sc_ref, nw_ref, h_ref):
    i = pl.program_id(0)

    def run(x, sh, sc):
        h_ref[...] = (_rms(x, nw_ref[...]) * (1.0 + sc) + sh).astype(BF16)

    @pl.when(i < NP_ROW_TILES)
    def _():
        run(xp_ref[...], _mod_rows(sh_ref, i)[0], _mod_rows(sc_ref, i)[0])

    @pl.when(i == NP_ROW_TILES)
    def _():
        run(xs_ref[...], _mod_rows(sh_ref, i)[1], _mod_rows(sc_ref, i)[1])


def _prompt_tile(i):
    return (jnp.minimum(i, NP_ROW_TILES - 1), 0)


def _x_specs(split):
    if split:
        return [pl.BlockSpec((ROW_TILE, D_MODEL), _prompt_tile),
                pl.BlockSpec((ROW_TILE, D_MODEL), lambda i: (0, 0))]
    return [pl.BlockSpec((ROW_TILE, D_MODEL), _prompt_tile),
            pl.BlockSpec((ROW_TILE, D_MODEL), lambda i: (NP_ROW_TILES, 0))]


def _mod_spec(section):
    return pl.BlockSpec((N_COND, D_MODEL), lambda i: (0, section))


def _norm1(xp, xs, split, mod, norm_w, layer):
    return pl.pallas_call(
        _norm1_kernel,
        grid=(N_ROW_TILES,),
        in_specs=_x_specs(split) + [_mod_spec(0), _mod_spec(1),
                                    pl.BlockSpec((None, 1, D_MODEL), lambda i: (layer, 0, 0))],
        out_specs=pl.BlockSpec((ROW_TILE, D_MODEL), lambda i: (i, 0)),
        out_shape=jax.ShapeDtypeStruct((N_ROWS, D_MODEL), BF16),
        compiler_params=_cparams(("arbitrary",)),
        name="norm_mix",
    )(xp, xs, mod, mod, norm_w.reshape(DEPTH, 1, D_MODEL))


def _route(logits):
    lane_i = _iota(logits.shape, 1)
    lane = lane_i.astype(F32)
    neg = jnp.float32(-jnp.inf)
    big = jnp.float32(1 << 20)
    first = lambda hit: jnp.min(jnp.where(hit, lane, big), axis=-1, keepdims=True)
    gl = jnp.where(lane_i < N_GROUPS, logits, neg)
    gm = jnp.max(gl, axis=-1, keepdims=True)
    g_top = 1.0 / jnp.sum(jnp.exp(gl - gm), axis=-1, keepdims=True)
    g_idx = first(gl == gm)
    eid = lane_i - N_GROUPS
    group_of_lane = (eid >> 3).astype(F32)
    in_group = (eid >= 0) & (eid < N_EXPERTS) & (group_of_lane == g_idx)
    el = jnp.where(in_group, logits, neg)
    m1 = jnp.max(el, axis=-1, keepdims=True)
    i1 = first(el == m1)
    denom = jnp.sum(jnp.exp(el - m1), axis=-1, keepdims=True)
    el2 = jnp.where(lane == i1, neg, el)
    m2 = jnp.max(el2, axis=-1, keepdims=True)
    i2 = first(el2 == m2)
    p1 = 1.0 / denom
    p2 = jnp.exp(m2 - m1) / denom
    tot = p1 + p2
    out = jnp.where(lane_i == 0, i1 - N_GROUPS, 0.0)
    out = jnp.where(lane_i == 1, i2 - N_GROUPS, out)
    out = jnp.where(lane_i == 2, g_top * p1 / tot, out)
    out = jnp.where(lane_i == 3, g_top * p2 / tot, out)
    return out


def _mid_kernel(xp_ref, xs_ref, o_ref, gt_ref, sh_ref, sc_ref, nw_ref, wr_ref, br_ref,
                xn_ref, h_ref, rt_ref):
    i = pl.program_id(0)

    def run(x, gt, sh, sc):
        xn = x + gt * o_ref[...]
        xn_ref[...] = xn
        h = _rms(xn, nw_ref[...]) * (1.0 + sc) + sh
        h_ref[...] = _pack_bf16_pairs(h)
        logits = _dot_split3(h, wr_ref[...]) + br_ref[...]
        rt_ref[...] = _route(logits)

    @pl.when(i < NP_ROW_TILES)
    def _():
        run(xp_ref[...], _mod_rows(gt_ref, i)[0], _mod_rows(sh_ref, i)[0], _mod_rows(sc_ref, i)[0])

    @pl.when(i == NP_ROW_TILES)
    def _():
        run(xs_ref[...], _mod_rows(gt_ref, i)[1], _mod_rows(sh_ref, i)[1], _mod_rows(sc_ref, i)[1])


def _mid(xp, xs, split, o, mod, norm_w, w_router, b_router, layer):
    row_spec = pl.BlockSpec((ROW_TILE, D_MODEL), lambda i: (i, 0))
    return pl.pallas_call(
        _mid_kernel,
        grid=(N_ROW_TILES,),
        in_specs=_x_specs(split) + [row_spec, _mod_spec(2), _mod_spec(3), _mod_spec(4),
                                    pl.BlockSpec((None, 1, D_MODEL), lambda i: (layer, 0, 0)),
                                    pl.BlockSpec((None, D_MODEL, LANES), lambda i: (layer, 0, 0)),
                                    pl.BlockSpec((None, 1, LANES), lambda i: (layer, 0, 0))],
        out_specs=[row_spec, pl.BlockSpec((ROW_TILE, D_MODEL // 2), lambda i: (i, 0)),
                   pl.BlockSpec((ROW_TILE, LANES), lambda i: (i, 0))],
        out_shape=[jax.ShapeDtypeStruct((N_ROWS, D_MODEL), F32),
                   jax.ShapeDtypeStruct((N_ROWS, D_MODEL // 2), jnp.int32),
                   jax.ShapeDtypeStruct((N_ROWS, LANES), F32)],
        compiler_params=_cparams(("arbitrary",)),
        name="norm_ffn_router",
    )(xp, xs, o, mod, mod, mod, norm_w.reshape(DEPTH, 1, D_MODEL), w_router, b_router)


def _row_gather(idx_ref, base, n, src_hbm, dst_ref, sem, wait):
    def body(r, carry):
        copy = pltpu.make_async_copy(src_hbm.at[pl.ds(idx_ref[base + r], 1), :],
                                     dst_ref.at[pl.ds(r, 1), :], sem)
        if wait:
            copy.wait()
        else:
            copy.start()
        return carry

    lax.fori_loop(0, n, body, 0, unroll=8)


def _combine_kernel(pos_ref, x_ref, rt_ref, gt_ref, y_hbm, nw_ref, sh_ref, sc_ref, *rest, final):
    if final:
        yp_ref, ys_ref, buf_ref, sem = rest
    else:
        xn_ref, h_ref, buf_ref, sem = rest
    i = pl.program_id(0)
    n = 2 * ROW_TILE
    slot = i % 2
    gather = lambda tile, s, wait: _row_gather(pos_ref, tile * n, n, y_hbm, buf_ref.at[s], sem.at[s], wait)

    @pl.when(i == 0)
    def _():
        gather(i, slot, False)

    @pl.when(i + 1 < pl.num_programs(0))
    def _():
        gather(i + 1, 1 - slot, False)

    gather(i, slot, True)
    rt = rt_ref[...]
    moe = rt[:, 2:3] * buf_ref[slot, 0:ROW_TILE, :] + rt[:, 3:4] * buf_ref[slot, ROW_TILE:n, :]

    def run(gt, sh, sc, out_ref):
        xn = x_ref[...] + gt * moe
        if final:
            out_ref[...] = _rms(xn, nw_ref[...])
        else:
            xn_ref[...] = xn
            h_ref[...] = (_rms(xn, nw_ref[...]) * (1.0 + sc) + sh).astype(BF16)

    @pl.when(i < NP_ROW_TILES)
    def _():
        run(_mod_rows(gt_ref, i)[0], _mod_rows(sh_ref, i)[0], _mod_rows(sc_ref, i)[0],
            yp_ref if final else None)

    @pl.when(i == NP_ROW_TILES)
    def _():
        run(_mod_rows(gt_ref, i)[1], _mod_rows(sh_ref, i)[1], _mod_rows(sc_ref, i)[1],
            ys_ref if final else None)


def _combine(pos, x, route, mod, y_sorted, norm_w, norm_layer, mod_next, final):
    row_spec = pl.BlockSpec((ROW_TILE, D_MODEL), lambda i, p: (i, 0))
    mod_spec = lambda s: pl.BlockSpec((N_COND, D_MODEL), lambda i, p: (0, s))
    if final:
        out_specs = [pl.BlockSpec((ROW_TILE, D_MODEL), lambda i, p: _prompt_tile(i)),
                     pl.BlockSpec((ROW_TILE, D_MODEL), lambda i, p: (0, 0))]
        out_shape = [jax.ShapeDtypeStruct((NP_ROWS, D_MODEL), F32),
                     jax.ShapeDtypeStruct((DEC_BATCH, D_MODEL), F32)]
    else:
        out_specs = [row_spec, row_spec]
        out_shape = [jax.ShapeDtypeStruct((N_ROWS, D_MODEL), F32),
                     jax.ShapeDtypeStruct((N_ROWS, D_MODEL), BF16)]
    grid_spec = pltpu.PrefetchScalarGridSpec(
        num_scalar_prefetch=1,
        grid=(N_ROW_TILES,),
        in_specs=[row_spec, pl.BlockSpec((ROW_TILE, LANES), lambda i, p: (i, 0)), mod_spec(5),
                  pl.BlockSpec(memory_space=pl.ANY),
                  pl.BlockSpec((None, 1, D_MODEL), lambda i, p: (norm_layer, 0, 0)),
                  mod_spec(0), mod_spec(1)],
        out_specs=out_specs,
        scratch_shapes=[pltpu.VMEM((2, 2 * ROW_TILE, D_MODEL), F32), pltpu.SemaphoreType.DMA((2,))],
    )
    return pl.pallas_call(
        functools.partial(_combine_kernel, final=final),
        grid_spec=grid_spec,
        out_shape=out_shape,
        compiler_params=_cparams(("arbitrary",)),
        name="moe_combine",
    )(pos, x, route, mod, y_sorted, norm_w, mod_next, mod_next)


def _expert_kernel(te_ref, src_ref, rows_ref, nv_ref, tok_ref, h_hbm, wg_ref, wu_ref, wd_ref, y_ref,
                   xb_ref, gx_ref, sem):
    i = pl.program_id(0)
    j = pl.program_id(1)
    rows = rows_ref[i]
    half_rows = MOE_TM // 2
    halves = [slice(0, half_rows), slice(half_rows, MOE_TM)]
    nxt = jnp.minimum(i + 1, MOE_TILES - 1)

    def row_copies(tile, half, wait):
        def body(r, carry):
            k = jnp.minimum(src_ref[tile] + half * half_rows + r, 2 * N_ROWS - 1)
            copy = pltpu.make_async_copy(h_hbm.at[pl.ds(tok_ref[k], 1), :],
                                         gx_ref.at[pl.ds(half * half_rows + r, 1), :], sem.at[half])
            if wait:
                copy.wait()
            else:
                copy.start()
            return carry
        lax.fori_loop(0, half_rows, body, 0, unroll=8)

    def gather(tile, wait):
        row_copies(tile, 0, wait)

        @pl.when(rows_ref[tile] > half_rows)
        def _():
            row_copies(tile, 1, wait)

    @pl.when((i == 0) & (j == 0) & (rows > 0))
    def _():
        gather(i, False)

    @pl.when((rows > 0) & (j == 0))
    def _():
        gather(i, True)

        def unpack(half):
            lo, hi = _unpack_bf16_pairs(gx_ref[halves[half], :])
            xb_ref[halves[half], 0:D_MODEL // 2] = lo
            xb_ref[halves[half], D_MODEL // 2:D_MODEL] = hi

        unpack(0)

        @pl.when(rows > half_rows)
        def _():
            unpack(1)

        @pl.when((i + 1 < MOE_TILES) & (rows_ref[nxt] > 0))
        def _():
            gather(nxt, False)

    def ffn(sl):
        n_out = 8
        x = xb_ref[sl, :]
        g = jnp.dot(x, wg_ref[...].astype(BF16), preferred_element_type=F32)
        u = jnp.dot(x, wu_ref[...].astype(BF16), preferred_element_type=F32)
        act = (_silu(g) * u).astype(BF16)
        for c in range(n_out):
            cols = slice(c * (D_MODEL // n_out), (c + 1) * (D_MODEL // n_out))
            part = jnp.dot(act, wd_ref[:, cols].astype(BF16), preferred_element_type=F32)

            @pl.when(j == 0)
            def _():
                y_ref[sl, cols] = part

            @pl.when(j > 0)
            def _():
                y_ref[sl, cols] += part

    @pl.when((rows > 0) & (rows <= half_rows))
    def _():
        ffn(halves[0])

    @pl.when(rows > half_rows)
    def _():
        ffn(slice(0, MOE_TM))


def _experts(tile_e, tile_src, tile_rows, n_valid, order_tok, h, w_gate, w_up, w_down, layer):
    fw = D_EXPERT // MOE_FC

    def f_idx(i, j, rows):
        return jnp.where(rows[i] > 0, j, MOE_FC - 1)

    grid_spec = pltpu.PrefetchScalarGridSpec(
        num_scalar_prefetch=5,
        grid=(MOE_TILES, MOE_FC),
        in_specs=[pl.BlockSpec(memory_space=pl.ANY),
                  pl.BlockSpec((None, None, D_MODEL, fw),
                               lambda i, j, te, src, rows, nv, tok: (layer, te[i], 0, f_idx(i, j, rows))),
                  pl.BlockSpec((None, None, D_MODEL, fw),
                               lambda i, j, te, src, rows, nv, tok: (layer, te[i], 0, f_idx(i, j, rows))),
                  pl.BlockSpec((None, None, fw, D_MODEL),
                               lambda i, j, te, src, rows, nv, tok: (layer, te[i], f_idx(i, j, rows), 0))],
        out_specs=pl.BlockSpec((MOE_TM, D_MODEL),
                               lambda i, j, te, src, rows, nv, tok: (jnp.minimum(i, nv[0] - 1), 0)),
        scratch_shapes=[pltpu.VMEM((MOE_TM, D_MODEL), BF16), pltpu.VMEM((MOE_TM, D_MODEL // 2), jnp.int32),
                        pltpu.SemaphoreType.DMA((2,))],
    )
    return pl.pallas_call(
        _expert_kernel,
        grid_spec=grid_spec,
        out_shape=jax.ShapeDtypeStruct((MOE_TILES * MOE_TM, D_MODEL), F32),
        compiler_params=pltpu.CompilerParams(dimension_semantics=("arbitrary", "arbitrary"),
                                             vmem_limit_bytes=MOE_VMEM_LIMIT),
        name="moe_experts",
    )(tile_e, tile_src, tile_rows, n_valid, order_tok, h, w_gate, w_up, w_down)


def _dispatch(route):
    flat_e = route[:, 0:2].astype(jnp.int32).reshape(-1)
    order = jnp.argsort(flat_e, stable=True).astype(jnp.int32)
    rank = jnp.argsort(order).astype(jnp.int32)
    onehot = flat_e[:, None] == jnp.arange(N_EXPERTS, dtype=jnp.int32)[None, :]
    counts = jnp.sum(onehot, axis=0, dtype=jnp.int32)
    start = jnp.cumsum(counts) - counts
    tiles = (counts + MOE_TM - 1) // MOE_TM
    tile_end = jnp.cumsum(tiles)
    tile_begin = tile_end - tiles
    shift = tile_begin * MOE_TM - start
    pos = rank + jnp.sum(jnp.where(onehot, shift[None, :], 0), axis=1, dtype=jnp.int32)
    pos = pos.reshape(N_ROW_TILES, ROW_TILE, 2).transpose(0, 2, 1).reshape(-1)
    n_valid = tile_end[-1]
    t = jnp.arange(MOE_TILES, dtype=jnp.int32)
    used = t < n_valid
    tile_e = jnp.minimum(jnp.sum(t[:, None] >= tile_end[None, :], axis=1, dtype=jnp.int32), N_EXPERTS - 1)
    tile_e = jnp.where(used, tile_e, tile_e[jnp.maximum(n_valid - 1, 0)])
    k_in_e = t - tile_begin[tile_e]
    tile_src = jnp.where(used, start[tile_e] + k_in_e * MOE_TM, 0)
    tile_rows = jnp.where(used, jnp.clip(counts[tile_e] - k_in_e * MOE_TM, 0, MOE_TM), 0)
    return tile_e, tile_src, tile_rows, n_valid.reshape(1), order // 2, pos


def _rwkv_features(r, k, v, xwa_tanh, xwa, xg_sig, prm):
    (w0, wup, a0, aup, gup, kkw, kaw, rkw) = prm
    lo = _iota(r.shape, 1) < A_HEAD
    w = w0 + _bdot(xwa_tanh, wup)
    w = -_softplus(-w) - 0.5
    lw = -jnp.exp(w)
    a = _sigmoid(a0 + _bdot(xwa, aup))
    g = _bdot(xg_sig, gup)
    kk = k * kkw
    kk = kk * lax.rsqrt(_half_sum(kk * kk, lo) + NORM_EPS)
    k2 = k * (1.0 + (a - 1.0) * kaw)
    bonus = _half_sum(r * k2 * rkw, lo) * v
    return lw, kk, k2, kk * a, g, bonus


def _rwkv_out(y, bonus, g, lnw, lnb):
    lo = _iota(y.shape, 1) < A_HEAD
    mu = _half_sum(y, lo) * (1.0 / A_HEAD)
    d = y - mu
    var = _half_sum(d * d, lo) * (1.0 / A_HEAD)
    yn = d * lax.rsqrt(var + A_LN_EPS) * lnw + lnb
    return ((yn + bonus) * g).astype(BF16)


def _rwkv_pairs_chunk(rs, vs, lws, kks, k2s, bs, ss):
    n = A_CHUNK
    pairs = range(len(rs))
    lo = _iota((n, LANES), 1) < A_HEAD
    stack = lambda x: jnp.concatenate([jnp.where(lo, x, 0.0), jnp.where(lo, 0.0, x)], axis=0)
    twice = lambda x: jnp.concatenate([x, x], axis=0)
    unstack = lambda x: jnp.where(lo, x[0:n, :], x[n:2 * n, :])
    cums = [_cumsum_rows(lw, n) for lw in lws]
    e_cums = [jnp.exp(cum) for cum in cums]
    e_invs = [jnp.exp(-cum) for cum in cums]
    r_decs = [rs[p] * e_cums[p] for p in pairs]
    kk_decs = [kks[p] * jnp.exp(cums[p] - lws[p]) for p in pairs]
    atts = [_bdot_nt(jnp.concatenate([stack(kk_decs[p]), stack(r_decs[p])], axis=0),
                     jnp.concatenate([stack(bs[p] * e_invs[p]), k2s[p] * e_invs[p]], axis=0))
            for p in pairs]
    t1s = [_bdot_nt(kk_decs[p], ss[p]) for p in pairs]
    y0s = [_bdot_nt(r_decs[p], ss[p]) for p in pairs]
    tr = _iota((2 * n, 2 * n), 0) & (n - 1)
    tc = _iota((2 * n, 2 * n), 1) & (n - 1)
    a_kkbs = [jnp.where(tr > tc, att[0:2 * n, 0:2 * n], 0.0) for att in atts]
    a_rbs = [jnp.where(tr >= tc, att[2 * n:4 * n, 0:2 * n], 0.0) for att in atts]
    tr = _iota((2 * n, n), 0) & (n - 1)
    tc = _iota((2 * n, n), 1)
    a_kkks = [jnp.where(tr > tc, att[0:2 * n, 2 * n:3 * n], 0.0) for att in atts]
    a_rks = [jnp.where(tr >= tc, att[2 * n:4 * n, 2 * n:3 * n], 0.0) for att in atts]
    kvs = [_bdot(a_kkks[p], vs[p]) for p in pairs]
    rvs = [_bdot(a_rks[p], vs[p]) for p in pairs]
    invs = _unit_lower_inverses(a_kkbs, 2 * n, n)
    u_sts = [_bdot(invs[p], twice(t1s[p]) + kvs[p]) for p in pairs]
    y_sts = [twice(y0s[p]) + rvs[p] - _bdot(a_rbs[p], u_sts[p]) for p in pairs]
    same = (_iota((LANES, LANES), 0) < A_HEAD) == (_iota((LANES, LANES), 1) < A_HEAD)
    new = []
    for p in pairs:
        cum_end = cums[p][n - 1:n, :]
        e_end = jnp.exp(cum_end - cums[p])
        upd = _bdot_tn(jnp.concatenate([vs[p], unstack(u_sts[p])], axis=0),
                       jnp.concatenate([k2s[p] * e_end, -(bs[p] * e_end)], axis=0))
        new.append(ss[p] * jnp.exp(cum_end) + jnp.where(same, upd, 0.0))
    return [unstack(y_st) for y_st in y_sts], new


def _rwkv_chunk_kernel(r_ref, k_ref, v_ref, xwa_ref, xg_ref, mr_ref, mk_ref, mv_ref, mwa_ref, mg_ref,
                       w0_ref, wup_ref, a0_ref, aup_ref, gup_ref, kkw_ref, kaw_ref, rkw_ref,
                       lnw_ref, lnb_ref, o_ref, s_out_ref, carry_ref, carry2_ref, s_ref):
    c = pl.program_id(2)
    n = A_CHUNK

    @pl.when(c == 0)
    def _():
        carry_ref[...] = jnp.zeros_like(carry_ref)
        carry2_ref[...] = jnp.zeros_like(carry2_ref)
        s_ref[...] = jnp.zeros_like(s_ref)

    def shifted(x_ref, mu_ref, c_ref, slot):
        x = x_ref[...]
        row0 = _iota(x.shape, 0) == 0
        prev = jnp.where(row0, c_ref[slot, 0:1, :], pltpu.roll(x, 1, 0))
        c_ref[slot, 0:1, :] = x[n - 1:n, :]
        return x + (prev - x) * mu_ref[...]

    r_all = shifted(r_ref, mr_ref, carry_ref, 0)
    k_all = shifted(k_ref, mk_ref, carry_ref, 1)
    v_all = shifted(v_ref, mv_ref, carry_ref, 2)
    xwa = shifted(xwa_ref, mwa_ref, carry2_ref, 0)
    xg_sig = _sigmoid(shifted(xg_ref, mg_ref, carry2_ref, 1))
    xwa_tanh = jnp.tanh(xwa)
    feats = []
    for i in range(RWKV_G):
        wl = slice(i * LANES, (i + 1) * LANES)
        prm = (w0_ref[:, wl], wup_ref[:, wl], a0_ref[:, wl], aup_ref[:, wl], gup_ref[:, wl],
               kkw_ref[:, wl], kaw_ref[:, wl], rkw_ref[:, wl])
        feats.append((r_all[:, wl], v_all[:, wl])
                     + _rwkv_features(r_all[:, wl], k_all[:, wl], v_all[:, wl], xwa_tanh, xwa, xg_sig, prm))
    rs, vs, lws, kks, k2s, bs, gs, bonuses = (list(col) for col in zip(*feats))
    ys, new = _rwkv_pairs_chunk(rs, vs, lws, kks, k2s, bs, [s_ref[i] for i in range(RWKV_G)])
    for i in range(RWKV_G):
        wl = slice(i * LANES, (i + 1) * LANES)
        s_ref[i] = new[i]
        o_ref[:, wl] = _rwkv_out(ys[i], bonuses[i], gs[i], lnw_ref[:, wl], lnb_ref[:, wl])

    @pl.when(c == pl.num_programs(2) - 1)
    def _():
        for i in range(RWKV_G):
            s_new = s_ref[i]
            s_out_ref[2 * i] = s_new[0:A_HEAD, 0:A_HEAD]
            s_out_ref[2 * i + 1] = pltpu.roll(s_new, A_HEAD, 1)[A_HEAD:LANES, 0:A_HEAD]


def _rwkv_params(p):
    f = lambda name: p[name].reshape(1, A_WIDTH)
    wup = jnp.pad(p['a_w_up'][0], ((0, LANES - A_LORA), (0, 0)))
    aup = jnp.pad(p['a_a_up'][0], ((LANES - A_LORA, 0), (0, 0)))
    return (f('a_w0'), wup, f('a_a0'), aup, p['a_g_up'][0], f('a_k_k'), f('a_k_a'), f('a_r_k'),
            f('a_ln_w'), f('a_ln_b'))


def _rwkv_prompt(z, p):
    nc = SEQ // A_CHUNK
    gw = RWKV_G * LANES
    per_tensor = A_WIDTH // gw
    lora_cb = 3 * A_WIDTH // LANES
    wide = [lambda h: h, lambda h: per_tensor + h, lambda h: 2 * per_tensor + h]
    zspecs = ([pl.BlockSpec((A_CHUNK, gw), lambda b, h, c, cb=cb: (b * nc + c, cb(h))) for cb in wide]
              + [pl.BlockSpec((A_CHUNK, LANES), lambda b, h, c, j=j: (b * nc + c, lora_cb + j))
                 for j in range(2)])
    muspecs = ([pl.BlockSpec((1, gw), lambda b, h, c, cb=cb: (0, cb(h))) for cb in wide]
               + [pl.BlockSpec((1, LANES), lambda b, h, c, j=j: (0, lora_cb + j)) for j in range(2)])
    vec = lambda: pl.BlockSpec((1, gw), lambda b, h, c: (0, h))
    mat = lambda: pl.BlockSpec((LANES, gw), lambda b, h, c: (0, h))
    in_specs = zspecs + muspecs + [vec(), mat(), vec(), mat(), mat(), vec(), vec(), vec(), vec(), vec()]
    mu = p['a_mu'].reshape(1, A_COLS)
    return pl.pallas_call(
        _rwkv_chunk_kernel,
        grid=(BATCH, per_tensor, nc),
        in_specs=in_specs,
        out_specs=[pl.BlockSpec((A_CHUNK, gw), lambda b, h, c: (b * nc + c, h)),
                   pl.BlockSpec((None, 2 * RWKV_G, A_HEAD, A_HEAD), lambda b, h, c: (b, h, 0, 0))],
        out_shape=[jax.ShapeDtypeStruct((N_ROWS, A_WIDTH), BF16),
                   jax.ShapeDtypeStruct((BATCH, A_HEADS, A_HEAD, A_HEAD), F32)],
        scratch_shapes=[pltpu.VMEM((3, 8, gw), F32), pltpu.VMEM((2, 8, LANES), F32),
                        pltpu.VMEM((RWKV_G, LANES, LANES), F32)],
        compiler_params=_cparams(("arbitrary", "arbitrary", "arbitrary")),
        name="rwkv7_chunk",
    )(z, z, z, z, z, mu, mu, mu, mu, mu, *_rwkv_params(p))


def _rwkv_step_kernel(r_ref, k_ref, v_ref, xwa_ref, xg_ref, pr_ref, pk_ref, pv_ref, pwa_ref, pg_ref,
                      mr_ref, mk_ref, mv_ref, mwa_ref, mg_ref,
                      w0_ref, wup_ref, a0_ref, aup_ref, gup_ref, kkw_ref, kaw_ref, rkw_ref,
                      lnw_ref, lnb_ref, s_in_ref, o_alias_ref, o_ref, s_out_ref, q_ref, y_ref):
    del o_alias_ref
    shifted = lambda x_ref, p_ref, mu_ref: x_ref[...] + (p_ref[...] - x_ref[...]) * mu_ref[...]
    r = shifted(r_ref, pr_ref, mr_ref)
    k = shifted(k_ref, pk_ref, mk_ref)
    v = shifted(v_ref, pv_ref, mv_ref)
    xwa = shifted(xwa_ref, pwa_ref, mwa_ref)
    xg = shifted(xg_ref, pg_ref, mg_ref)
    prm = (w0_ref[...], wup_ref[...], a0_ref[...], aup_ref[...], gup_ref[...], kkw_ref[...],
           kaw_ref[...], rkw_ref[...])
    lw, kk, k2, b, g, bonus = _rwkv_features(r, k, v, jnp.tanh(xwa), xwa, _sigmoid(xg), prm)
    for slot, x in enumerate((kk, jnp.exp(lw), b, k2, r, v)):
        q_ref[slot] = x.T

    def body(vi, carry):
        for hh in range(2):
            chan = slice(hh * A_HEAD, (hh + 1) * A_HEAD)
            s = s_in_ref[hh, vi]
            sa = jnp.sum(s * q_ref[0, chan, :], axis=0, keepdims=True)
            s = (s * q_ref[1, chan, :] - sa * q_ref[2, chan, :]
                 + q_ref[5, pl.ds(hh * A_HEAD + vi, 1), :] * q_ref[3, chan, :])
            s_out_ref[hh, vi] = s
            y_ref[pl.ds(hh * A_HEAD + vi, 1), :] = jnp.sum(s * q_ref[4, chan, :], axis=0, keepdims=True)
        return carry

    lax.fori_loop(0, A_HEAD, body, 0, unroll=2)
    o_ref[...] = _rwkv_out(y_ref[...].T, bonus, g, lnw_ref[...], lnb_ref[...])


def _rwkv_sample(z, shift, s0, p, o_all):
    hp_n = A_HEADS // 2
    rb = NP_ROWS // DEC_BATCH
    cols = [lambda h: h, lambda h: hp_n + h, lambda h: 2 * hp_n + h, lambda h: 3 * hp_n,
            lambda h: 3 * hp_n + 1]
    zspec = lambda cb: pl.BlockSpec((DEC_BATCH, LANES), lambda h: (rb, cb(h)))
    pspec = lambda cb: pl.BlockSpec((DEC_BATCH, LANES), lambda h: (0, cb(h)))
    muspec = lambda cb: pl.BlockSpec((1, LANES), lambda h: (0, cb(h)))
    vec = lambda: pl.BlockSpec((1, LANES), lambda h: (0, h))
    mat = lambda: pl.BlockSpec((LANES, LANES), lambda h: (0, h))
    sspec = pl.BlockSpec((2, A_HEAD, A_HEAD, DEC_BATCH), lambda h: (h, 0, 0, 0))
    in_specs = ([zspec(cb) for cb in cols] + [pspec(cb) for cb in cols] + [muspec(cb) for cb in cols]
                + [vec(), mat(), vec(), mat(), mat(), vec(), vec(), vec(), vec(), vec()]
                + [sspec, pl.BlockSpec(memory_space=pl.ANY)])
    mu = p['a_mu'].reshape(1, A_COLS)
    n_in = len(in_specs)
    return pl.pallas_call(
        _rwkv_step_kernel,
        grid=(hp_n,),
        in_specs=in_specs,
        out_specs=[pl.BlockSpec((DEC_BATCH, LANES), lambda h: (rb, h)), sspec],
        out_shape=[jax.ShapeDtypeStruct((N_ROWS, A_WIDTH), BF16),
                   jax.ShapeDtypeStruct((A_HEADS, A_HEAD, A_HEAD, DEC_BATCH), F32)],
        scratch_shapes=[pltpu.VMEM((6, LANES, DEC_BATCH), F32), pltpu.VMEM((LANES, DEC_BATCH), F32)],
        input_output_aliases={n_in - 1: 0},
        compiler_params=_cparams(("arbitrary",)),
        name="rwkv7_step",
    )(z, z, z, z, z, shift, shift, shift, shift, shift, mu, mu, mu, mu, mu, *_rwkv_params(p), s0, o_all)


def _gdn_gates(gates, scal, h):
    lane = _iota(gates.shape, 1)
    a_log = scal[0:1, :]
    dt_bias = scal[1:2, :]
    g_all = -jnp.exp(a_log) * _softplus(gates + dt_bias)
    beta = jnp.sum(jnp.where(lane == h, _sigmoid(gates), 0.0), axis=-1, keepdims=True)
    g = jnp.sum(jnp.where(lane == B_HEADS + h, g_all, 0.0), axis=-1, keepdims=True)
    return beta, g


def _gdn_qkv(qc, kc, vc):
    q = _silu(qc)
    k = _silu(kc)
    q = q * lax.rsqrt(jnp.sum(q * q, axis=-1, keepdims=True) + NORM_EPS) * (B_HEAD ** -0.5)
    k = k * lax.rsqrt(jnp.sum(k * k, axis=-1, keepdims=True) + NORM_EPS)
    return q, k, _silu(vc)


def _gdn_out(o, zg, nw):
    return (_rms(o, nw) * _silu(zg)).astype(BF16)


def _gdn_heads_chunk(qs, ks, vs, betas, gs, ss):
    n = B_CHUNK
    heads = range(len(qs))
    tr = _iota((n, n), 0)
    tc = _iota((n, n), 1)
    lane0 = jnp.where(_iota((n, LANES), 1) == 0, 1.0, 0.0)
    cums = [_cumsum_rows(jnp.broadcast_to(g, (n, LANES)), n) for g in gs]
    cum_rows = [_fdot_nt(lane0, cum) for cum in cums]
    grams = [_bdot_nt(jnp.concatenate([k, q], axis=0), k) for q, k in zip(qs, ks)]
    dmats = [jnp.where(tr >= tc, jnp.exp(jnp.where(tr >= tc, cum[:, 0:n] - cr, 0.0)), 0.0)
             for cum, cr in zip(cums, cum_rows)]
    lows = [jnp.where(tr > tc, betas[h] * dmats[h] * grams[h][0:n, :], 0.0) for h in heads]
    invs = _unit_lower_inverses(lows, n, n)
    e_cums = [jnp.exp(cum) for cum in cums]
    sols = [_bdot(invs[h], jnp.concatenate([betas[h] * vs[h], betas[h] * e_cums[h] * ks[h]], axis=1))
            for h in heads]
    wks = [_bdot_nt(sols[h][:, LANES:2 * LANES], ss[h]) for h in heads]
    os_ = [_bdot_nt(qs[h] * e_cums[h], ss[h]) for h in heads]
    us = [sols[h][:, 0:LANES] - wks[h] for h in heads]
    os_ = [os_[h] + _bdot(dmats[h] * grams[h][n:2 * n, :], us[h]) for h in heads]
    new = []
    for h in heads:
        cum_end = cums[h][n - 1:n, :]
        new.append(ss[h] * jnp.exp(cum_end) + _bdot_tn(us[h], ks[h] * jnp.exp(cum_end - cums[h])))
    return os_, new


def _gdn_chunk_kernel(*refs):
    nb = GDN_G // 2
    q_refs, k_refs, v_refs, zg_refs = (refs[i * nb:(i + 1) * nb] for i in range(4))
    (gt_ref, cwq_ref, cwk_ref, cwv_ref, scal_ref, nw_ref, o_ref, s_out_ref, carry_ref,
     s_ref) = refs[4 * nb:]
    hg = pl.program_id(1)
    c = pl.program_id(2)
    n = B_CHUNK

    @pl.when(c == 0)
    def _():
        carry_ref[...] = jnp.zeros_like(carry_ref)
        s_ref[...] = jnp.zeros_like(s_ref)

    def conv(x, w, slot, i):
        ext = jnp.concatenate([carry_ref[slot, i], x], axis=0)
        carry_ref[slot, i] = x[n - 8:n, :]
        y = ext[8:8 + n, :] * w[CONV_W - 1:CONV_W, :]
        for j in range(CONV_W - 1):
            y = y + pltpu.roll(ext, CONV_W - 1 - j, 0)[8:8 + n, :] * w[j:j + 1, :]
        return y

    gates = gt_ref[...]
    scal = scal_ref[...]
    qs, ks, vs, betas, gs = [], [], [], [], []
    for i in range(GDN_G):
        sl = slice((i % 2) * LANES, (i % 2 + 1) * LANES)
        wl = slice(i * LANES, (i + 1) * LANES)
        q, k, v = _gdn_qkv(conv(q_refs[i // 2][:, sl], cwq_ref[:, wl], 0, i),
                           conv(k_refs[i // 2][:, sl], cwk_ref[:, wl], 1, i),
                           conv(v_refs[i // 2][:, sl], cwv_ref[:, wl], 2, i))
        beta, g = _gdn_gates(gates, scal, hg * GDN_G + i)
        qs.append(q), ks.append(k), vs.append(v), betas.append(beta), gs.append(g)
    outs, new = _gdn_heads_chunk(qs, ks, vs, betas, gs, [s_ref[i] for i in range(GDN_G)])
    for i in range(GDN_G):
        sl = slice((i % 2) * LANES, (i % 2 + 1) * LANES)
        s_ref[i] = new[i]
        o_ref[:, i * LANES:(i + 1) * LANES] = _gdn_out(outs[i], zg_refs[i // 2][:, sl], nw_ref[...])

    @pl.when(c == pl.num_programs(2) - 1)
    def _():
        s_out_ref[...] = s_ref[...]


def _gdn_scalars(p):
    scal = jnp.zeros((8, LANES), F32)
    scal = scal.at[0, B_HEADS:2 * B_HEADS].set(p['b_A_log'][0])
    return scal.at[1, B_HEADS:2 * B_HEADS].set(p['b_dt_bias'][0])


def _gdn_prompt(z, p):
    nc = SEQ // B_CHUNK
    nb = GDN_G // 2
    wide = 2 * LANES
    cb0 = A_COLS // wide
    per_tensor = B_WIDTH // wide

    def zspecs(t):
        return [pl.BlockSpec((B_CHUNK, wide),
                             lambda b, h, c, i=i: (b * nc + c, cb0 + t * per_tensor + h * nb + i))
                for i in range(nb)]

    gw = GDN_G * LANES
    cwspec = lambda t: pl.BlockSpec((None, CONV_W, gw), lambda b, h, c: (0, 0, t * (B_WIDTH // gw) + h))
    in_specs = (zspecs(0) + zspecs(1) + zspecs(2) + zspecs(3)
                + [pl.BlockSpec((B_CHUNK, LANES), lambda b, h, c: (b * nc + c, (A_COLS + 4 * B_WIDTH) // LANES)),
                   cwspec(0), cwspec(1), cwspec(2),
                   pl.BlockSpec((8, LANES), lambda b, h, c: (0, 0)),
                   pl.BlockSpec((1, LANES), lambda b, h, c: (0, 0))])
    cw = p['b_conv_w']
    return pl.pallas_call(
        _gdn_chunk_kernel,
        grid=(BATCH, B_HEADS // GDN_G, nc),
        in_specs=in_specs,
        out_specs=[pl.BlockSpec((B_CHUNK, gw), lambda b, h, c: (b * nc + c, h)),
                   pl.BlockSpec((None, GDN_G, B_HEAD, B_HEAD), lambda b, h, c: (b, h, 0, 0))],
        out_shape=[jax.ShapeDtypeStruct((N_ROWS, B_WIDTH), BF16),
                   jax.ShapeDtypeStruct((BATCH, B_HEADS, B_HEAD, B_HEAD), F32)],
        scratch_shapes=[pltpu.VMEM((3, GDN_G, 8, LANES), F32), pltpu.VMEM((GDN_G, B_HEAD, B_HEAD), F32)],
        compiler_params=_cparams(("arbitrary", "arbitrary", "arbitrary")),
        name="gdn_chunk",
    )(*([z] * (4 * nb + 1)), cw, cw, cw, _gdn_scalars(p), p['b_norm_w'])


def _gdn_step_kernel(q_ref, k_ref, v_ref, zg_ref, gt_ref, bq_ref, bk_ref, bv_ref, cwq_ref, cwk_ref,
                     cwv_ref, scal_ref, nw_ref, s_in_ref, o_alias_ref, o_ref, s_out_ref, x_ref, y_ref):
    del o_alias_ref
    h = pl.program_id(0)
    nb = q_ref.shape[0]

    def conv(x_ref_, buf_ref, w_ref):
        y = x_ref_[...] * w_ref[CONV_W - 1:CONV_W, :]
        for j in range(CONV_W - 1):
            y = y + buf_ref[:, j, :] * w_ref[j:j + 1, :]
        return y

    q, k, v = _gdn_qkv(conv(q_ref, bq_ref, cwq_ref), conv(k_ref, bk_ref, cwk_ref),
                       conv(v_ref, bv_ref, cwv_ref))
    beta, g = _gdn_gates(gt_ref[...], scal_ref[...], h)
    e_g = jnp.exp(g)
    x_ref[0] = k
    x_ref[1] = q
    x_ref[2] = beta * v
    x_ref[3] = jnp.broadcast_to(beta * e_g, (nb, LANES))
    x_ref[4] = jnp.broadcast_to(e_g, (nb, LANES))
    x_ref[5] = jnp.broadcast_to(jnp.sum(q * k, axis=-1, keepdims=True), (nb, LANES))

    def body(i, carry):
        row = lambda slot: x_ref[slot, pl.ds(i, 1), :]
        s = s_in_ref[i]
        k_row = row(0)
        e_col = row(4)[:, 0:1]
        s_k = jnp.sum(s * k_row, axis=1, keepdims=True)
        s_q = jnp.sum(s * row(1), axis=1, keepdims=True)
        u = _row_to_col(row(2), B_HEAD) - row(3)[:, 0:1] * s_k
        o_col = e_col * s_q + row(5)[:, 0:1] * u
        s_out_ref[i] = s * e_col + u * k_row
        y_ref[pl.ds(i, 1), :] = _col_to_row(o_col, B_HEAD)
        return carry

    lax.fori_loop(0, nb, body, 0, unroll=8)
    o_ref[...] = _gdn_out(y_ref[...], zg_ref[...], nw_ref[...])


GDN_STEP_BT = 64


def _gdn_sample(z, conv_buf, s0, p, o_alias):
    bt = GDN_STEP_BT
    rb = NP_ROWS // bt
    cb0 = A_COLS // LANES
    zspec = lambda off: pl.BlockSpec((bt, LANES), lambda h, b: (rb + b, cb0 + off + h))
    bspec = lambda off: pl.BlockSpec((bt, CONV_W - 1, LANES), lambda h, b: (b, 0, off + h))
    cwspec = lambda off: pl.BlockSpec((None, CONV_W, LANES), lambda h, b: (0, 0, off + h))
    sspec = pl.BlockSpec((bt, None, B_HEAD, B_HEAD), lambda h, b: (b, h, 0, 0))
    in_specs = [zspec(0), zspec(B_HEADS), zspec(2 * B_HEADS), zspec(3 * B_HEADS),
                pl.BlockSpec((bt, LANES), lambda h, b: (rb + b, cb0 + 4 * B_HEADS)),
                bspec(0), bspec(B_HEADS), bspec(2 * B_HEADS),
                cwspec(0), cwspec(B_HEADS), cwspec(2 * B_HEADS),
                pl.BlockSpec((8, LANES), lambda h, b: (0, 0)),
                pl.BlockSpec((1, LANES), lambda h, b: (0, 0)),
                sspec, pl.BlockSpec(memory_space=pl.ANY)]
    cw = p['b_conv_w']
    return pl.pallas_call(
        _gdn_step_kernel,
        grid=(B_HEADS, DEC_BATCH // bt),
        in_specs=in_specs,
        out_specs=[pl.BlockSpec((bt, LANES), lambda h, b: (rb + b, h)), sspec],
        out_shape=[jax.ShapeDtypeStruct((N_ROWS, B_WIDTH), BF16),
                   jax.ShapeDtypeStruct((DEC_BATCH, B_HEADS, B_HEAD, B_HEAD), F32)],
        scratch_shapes=[pltpu.VMEM((6, bt, LANES), F32), pltpu.VMEM((bt, LANES), F32)],
        input_output_aliases={14: 0},
        compiler_params=_cparams(("arbitrary", "arbitrary")),
        name="gdn_step",
    )(z, z, z, z, z, conv_buf, conv_buf, conv_buf, cw, cw, cw, _gdn_scalars(p), p['b_norm_w'], s0, o_alias)


def _rglru_gates(xc, gate, wr, br, wi, bi, lam):
    def blockdiag(w):
        if w.ndim == 2:
            return _bdot(xc, w)
        return jnp.concatenate([_bdot(xc[:, g * LANES:(g + 1) * LANES], w[g]) for g in range(w.shape[0])],
                               axis=1)

    r = _sigmoid(blockdiag(wr) + br)
    i = _sigmoid(blockdiag(wi) + bi)
    log_a = -RGLRU_C * r * _softplus(-lam)
    a = jnp.exp(log_a)
    b = jnp.sqrt(1.0 - jnp.exp(2.0 * log_a)) * i * xc
    return a, b, _gelu_tanh(gate)


def _rglru_chunk_kernel(x_ref, g_ref, cw_ref, cb_ref, wr_ref, br_ref, wi_ref, bi_ref, lam_ref,
                        o_ref, h_out_ref, carry_ref, h_ref):
    c = pl.program_id(2)
    n = C_TILE

    @pl.when(c == 0)
    def _():
        carry_ref[...] = jnp.zeros_like(carry_ref)
        h_ref[...] = jnp.zeros_like(h_ref)

    x = x_ref[...]
    ext = jnp.concatenate([carry_ref[...], x], axis=0)
    carry_ref[...] = x[n - 8:n, :]
    xc = ext[8:8 + n, :] * cw_ref[CONV_W - 1:CONV_W, :] + cb_ref[...]
    for j in range(CONV_W - 1):
        xc = xc + pltpu.roll(ext, CONV_W - 1 - j, 0)[8:8 + n, :] * cw_ref[j:j + 1, :]
    a, b, gate = _rglru_gates(xc, g_ref[...], wr_ref[...], br_ref[...], wi_ref[...], bi_ref[...],
                              lam_ref[...])
    row = _iota(a.shape, 0)
    s = 1
    while s < n:
        keep = row >= s
        a_sh = jnp.where(keep, pltpu.roll(a, s, 0), 1.0)
        b_sh = jnp.where(keep, pltpu.roll(b, s, 0), 0.0)
        b = a * b_sh + b
        a = a * a_sh
        s *= 2
    hs = a * h_ref[0:1, :] + b
    h_ref[0:1, :] = hs[n - 1:n, :]
    o_ref[...] = (hs * gate).astype(BF16)

    @pl.when(c == pl.num_programs(2) - 1)
    def _():
        h_out_ref[...] = hs[n - 1:n, :]


def _rglru_prompt(z, p):
    nc = SEQ // C_TILE
    gw = RGLRU_G * LANES
    per_tensor = C_WIDTH // gw
    vec = lambda: pl.BlockSpec((1, gw), lambda b, j, c: (0, j))
    blk = lambda: pl.BlockSpec((None, RGLRU_G, LANES, LANES), lambda b, j, c: (0, j, 0, 0))
    in_specs = [pl.BlockSpec((C_TILE, gw), lambda b, j, c: (b * nc + c, j)),
                pl.BlockSpec((C_TILE, gw), lambda b, j, c: (b * nc + c, per_tensor + j)),
                pl.BlockSpec((None, CONV_W, gw), lambda b, j, c: (0, 0, j)),
                vec(), blk(), vec(), blk(), vec(), vec()]
    return pl.pallas_call(
        _rglru_chunk_kernel,
        grid=(BATCH, per_tensor, nc),
        in_specs=in_specs,
        out_specs=[pl.BlockSpec((C_TILE, gw), lambda b, j, c: (b * nc + c, j)),
                   pl.BlockSpec((None, 1, gw), lambda b, j, c: (b, 0, j))],
        out_shape=[jax.ShapeDtypeStruct((N_ROWS, C_WIDTH), BF16),
                   jax.ShapeDtypeStruct((BATCH, 1, C_WIDTH), F32)],
        scratch_shapes=[pltpu.VMEM((8, gw), F32), pltpu.VMEM((8, gw), F32)],
        compiler_params=_cparams(("arbitrary", "arbitrary", "arbitrary")),
        name="rglru_chunk",
    )(z, z, p['c_conv_w'], p['c_conv_b'], p['c_w_r'], p['c_b_r'], p['c_w_i'], p['c_b_i'], p['c_lambda'])


def _rglru_step_kernel(x_ref, g_ref, buf_ref, h0_ref, cw_ref, cb_ref, wr_ref, br_ref, wi_ref, bi_ref,
                       lam_ref, o_alias_ref, o_ref, h_out_ref):
    del o_alias_ref
    xc = x_ref[...] * cw_ref[CONV_W - 1:CONV_W, :] + cb_ref[...]
    for j in range(CONV_W - 1):
        xc = xc + buf_ref[:, j, :] * cw_ref[j:j + 1, :]
    a, b, gate = _rglru_gates(xc, g_ref[...], wr_ref[...], br_ref[...], wi_ref[...], bi_ref[...],
                              lam_ref[...])
    hs = a * h0_ref[...] + b
    h_out_ref[...] = hs
    o_ref[...] = (hs * gate).astype(BF16)


def _rglru_sample(z, conv_buf, h0, p, o_alias):
    rb = NP_ROWS // DEC_BATCH
    vec = lambda: pl.BlockSpec((1, LANES), lambda j: (0, j))
    blk = lambda: pl.BlockSpec((None, None, LANES, LANES), lambda j: (0, j, 0, 0))
    in_specs = [pl.BlockSpec((DEC_BATCH, LANES), lambda j: (rb, j)),
                pl.BlockSpec((DEC_BATCH, LANES), lambda j: (rb, C_BLOCKS + j)),
                pl.BlockSpec((DEC_BATCH, CONV_W - 1, LANES), lambda j: (0, 0, j)),
                pl.BlockSpec((DEC_BATCH, LANES), lambda j: (0, j)),
                pl.BlockSpec((None, CONV_W, LANES), lambda j: (0, 0, j)),
                vec(), blk(), vec(), blk(), vec(), vec(), pl.BlockSpec(memory_space=pl.ANY)]
    return pl.pallas_call(
        _rglru_step_kernel,
        grid=(C_BLOCKS,),
        in_specs=in_specs,
        out_specs=[pl.BlockSpec((DEC_BATCH, LANES), lambda j: (rb, j)),
                   pl.BlockSpec((DEC_BATCH, LANES), lambda j: (0, j))],
        out_shape=[jax.ShapeDtypeStruct((N_ROWS, C_WIDTH), BF16),
                   jax.ShapeDtypeStruct((DEC_BATCH, C_WIDTH), F32)],
        input_output_aliases={11: 0},
        compiler_params=_cparams(("arbitrary",)),
        name="rglru_step",
    )(z, z, conv_buf, h0, p['c_conv_w'], p['c_conv_b'], p['c_w_r'], p['c_b_r'], p['c_w_i'], p['c_b_i'],
      p['c_lambda'], o_alias)


def _hgrn_gates(f, lb_ref, layer):
    raw = lb_ref[...]
    e = jnp.exp(raw - jnp.max(raw, axis=0, keepdims=True))
    sm = e / jnp.sum(e, axis=0, keepdims=True)
    lb = jnp.sum(sm[0:layer + 1, :], axis=0, keepdims=True) - sm[0:1, :]
    return lb + (1.0 - lb) * _sigmoid(f)


def _hgrn_out(o, og, nw):
    return (_rms(o, nw) * _silu(og)).astype(BF16)


def _hgrn_chunk_kernel(q_ref, f_ref, i_ref, og_ref, lb_ref, nw_ref, o_ref, s_out_ref, st_ref, *, layer):
    c = pl.program_id(2)
    n = D_TILE
    m = D_SUB

    @pl.when(c == 0)
    def _():
        st_ref[...] = jnp.zeros_like(st_ref)

    fg_all = _hgrn_gates(f_ref[...], lb_ref, layer)
    q_all = _silu(q_ref[...])
    cum_all = _cumsum_rows(jnp.log(fg_all), m)
    tr = _iota((m, LANES), 0)
    subs = [slice(i * m, (i + 1) * m) for i in range(n // m)]
    heads = range(HGRN_G)
    lanes = [slice(h * LANES, (h + 1) * LANES) for h in heads]
    qs_ = [q_all[:, wl] for wl in lanes]
    ks_ = [1.0 - fg_all[:, wl] for wl in lanes]
    vs_ = [i_ref[:, wl] for wl in lanes]
    cums = [cum_all[:, wl] for wl in lanes]
    ends = [[cums[h][sl][m - 1:m, :] for sl in subs] for h in heads]
    deltas = [[_bdot_tn(vs_[h][sl], ks_[h][sl] * jnp.exp(c_end - cums[h][sl]))
               for sl, c_end in zip(subs, ends[h])] for h in heads]
    states = []
    for h in heads:
        st = st_ref[h]
        per_sub = []
        for c_end, delta in zip(ends[h], deltas[h]):
            per_sub.append(st)
            st = st * jnp.exp(c_end) + delta
        st_ref[h] = st
        states.append(per_sub)
    outs = [[_bdot_nt((qs_[h] * jnp.exp(cums[h]))[sl], s0) for sl, s0 in zip(subs, states[h])]
            for h in heads]
    for h in heads:
        for i, sl in enumerate(subs):
            q, k, v, cs = qs_[h][sl], ks_[h][sl], vs_[h][sl], cums[h][sl]
            o = outs[h][i]
            for j in range(m):
                w = jnp.exp(jnp.minimum(cs - cs[j:j + 1, :], 0.0))
                att = jnp.sum(q * w * k[j:j + 1, :], axis=-1, keepdims=True)
                o = o + jnp.where(tr >= j, att, 0.0) * v[j:j + 1, :]
            outs[h][i] = o
        o_ref[:, lanes[h]] = _hgrn_out(jnp.concatenate(outs[h], axis=0), og_ref[:, lanes[h]], nw_ref[...])

    @pl.when(c == pl.num_programs(2) - 1)
    def _():
        for h in heads:
            s_out_ref[h] = st_ref[h].T


def _hgrn_prompt(z, p, layer):
    nc = SEQ // D_TILE
    gw = HGRN_G * LANES
    cb0 = C_COLS // gw
    per_tensor = D_WIDTH // gw
    zspec = lambda t: pl.BlockSpec((D_TILE, gw), lambda b, h, c: (b * nc + c, cb0 + t * per_tensor + h))
    in_specs = [zspec(0), zspec(1), zspec(2), zspec(3),
                pl.BlockSpec((DEPTH, gw), lambda b, h, c: (0, h)),
                pl.BlockSpec((1, LANES), lambda b, h, c: (0, 0))]
    return pl.pallas_call(
        functools.partial(_hgrn_chunk_kernel, layer=layer),
        grid=(BATCH, per_tensor, nc),
        in_specs=in_specs,
        out_specs=[pl.BlockSpec((D_TILE, gw), lambda b, h, c: (b * nc + c, h)),
                   pl.BlockSpec((None, HGRN_G, LANES, LANES), lambda b, h, c: (b, h, 0, 0))],
        out_shape=[jax.ShapeDtypeStruct((N_ROWS, D_WIDTH), BF16),
                   jax.ShapeDtypeStruct((BATCH, D_HEADS, LANES, LANES), F32)],
        scratch_shapes=[pltpu.VMEM((HGRN_G, LANES, LANES), F32)],
        compiler_params=_cparams(("arbitrary", "arbitrary", "arbitrary")),
        name="hgrn2_chunk",
    )(z, z, z, z, p['d_lb'], p['d_norm_w'])


def _hgrn_step_kernel(q_ref, f_ref, i_ref, og_ref, lb_ref, nw_ref, s_in_ref, o_alias_ref, o_ref,
                      s_out_ref, x_ref, y_ref, *, layer):
    del o_alias_ref
    nb = q_ref.shape[0]
    x_ref[0] = _hgrn_gates(f_ref[...], lb_ref, layer)
    x_ref[1] = _silu(q_ref[...])

    def body(i, carry):
        fg = _row_to_col(x_ref[0, pl.ds(i, 1), :], LANES)
        qc = _row_to_col(x_ref[1, pl.ds(i, 1), :], LANES)
        s = s_in_ref[i] * fg + (1.0 - fg) * i_ref[pl.ds(i, 1), :]
        s_out_ref[i] = s
        y_ref[pl.ds(i, 1), :] = jnp.sum(s * qc, axis=0, keepdims=True)
        return carry

    lax.fori_loop(0, nb, body, 0, unroll=8)
    o_ref[...] = _hgrn_out(y_ref[...], og_ref[...], nw_ref[...])


HGRN_STEP_BT = 64


def _hgrn_sample(z, s0, p, layer, o_alias):
    bt = HGRN_STEP_BT
    rb = NP_ROWS // bt
    cb0 = C_COLS // LANES
    zspec = lambda off: pl.BlockSpec((bt, LANES), lambda h, b: (rb + b, cb0 + off + h))
    sspec = pl.BlockSpec((bt, None, LANES, LANES), lambda h, b: (b, h, 0, 0))
    in_specs = [zspec(0), zspec(D_HEADS), zspec(2 * D_HEADS), zspec(3 * D_HEADS),
                pl.BlockSpec((DEPTH, LANES), lambda h, b: (0, h)),
                pl.BlockSpec((1, LANES), lambda h, b: (0, 0)),
                sspec, pl.BlockSpec(memory_space=pl.ANY)]
    return pl.pallas_call(
        functools.partial(_hgrn_step_kernel, layer=layer),
        grid=(D_HEADS, DEC_BATCH // bt),
        in_specs=in_specs,
        out_specs=[pl.BlockSpec((bt, LANES), lambda h, b: (rb + b, h)), sspec],
        out_shape=[jax.ShapeDtypeStruct((N_ROWS, D_WIDTH), BF16),
                   jax.ShapeDtypeStruct((DEC_BATCH, D_HEADS, LANES, LANES), F32)],
        scratch_shapes=[pltpu.VMEM((2, bt, LANES), F32), pltpu.VMEM((bt, LANES), F32)],
        input_output_aliases={7: 0},
        compiler_params=_cparams(("arbitrary", "arbitrary")),
        name="hgrn2_step",
    )(z, z, z, z, p['d_lb'], p['d_norm_w'], s0, o_alias)


def kernel(x_prompt, x_sample, c_prompt, c_sample, state_a_shift, state_a_wkv, state_b_conv, state_b_delta,
           state_c_conv, state_c_h, state_d_s, w_mod, b_mod, norm_mix, norm_ffn, final_norm, w_in_even,
           w_out_even, a_mu, a_w0, a_w_up, a_a0, a_a_up, a_g_up, a_k_k, a_k_a, a_r_k, a_ln_w, a_ln_b,
           b_conv_w, b_A_log, b_dt_bias, b_norm_w, w_in_odd, w_out_odd, c_conv_w, c_conv_b, c_w_r, c_b_r,
           c_w_i, c_b_i, c_lambda, d_lb, d_norm_w, w_router_group, b_router_group, w_router_expert,
           b_router_expert, w_expert_gate, w_expert_up, w_expert_down):
    p = dict(a_mu=a_mu, a_w0=a_w0, a_w_up=a_w_up, a_a0=a_a0, a_a_up=a_a_up, a_g_up=a_g_up, a_k_k=a_k_k,
             a_k_a=a_k_a, a_r_k=a_r_k, a_ln_w=a_ln_w, a_ln_b=a_ln_b, b_conv_w=b_conv_w, b_A_log=b_A_log,
             b_dt_bias=b_dt_bias, b_norm_w=b_norm_w, c_conv_w=c_conv_w, c_conv_b=c_conv_b, c_w_r=c_w_r,
             c_b_r=c_b_r, c_w_i=c_w_i, c_b_i=c_b_i, c_lambda=c_lambda, d_lb=d_lb, d_norm_w=d_norm_w)
    cond = jnp.concatenate([c_sample, c_prompt], axis=0)
    w_router = jnp.pad(jnp.concatenate([w_router_group, w_router_expert], axis=-1),
                       ((0, 0), (0, 0), (0, LANES - N_GROUPS - N_EXPERTS)))
    b_router = jnp.pad(jnp.concatenate([b_router_group, b_router_expert], axis=-1),
                       ((0, 0), (0, LANES - N_GROUPS - N_EXPERTS))).reshape(DEPTH, 1, LANES)
    mods = [_matmul([cond], w_mod, layer, bias=b_mod, silu_in=True, tm=N_COND, name="adaln_mod")
            for layer in range(DEPTH)]

    xp = x_prompt.reshape(NP_ROWS, D_MODEL)
    xs = x_sample.reshape(DEC_BATCH, D_MODEL)
    split = True
    h = _norm1(xp, xs, True, mods[0], norm_mix, 0)
    new_p, new_s = {}, {}
    for layer in range(DEPTH):
        mod = mods[layer]
        if layer % 2 == 0:
            z = _matmul([h], jnp.swapaxes(w_in_even, 1, 2), 0, w_transposed=True, name="w_in_even")
            oa, new_p['a_wkv'] = _rwkv_prompt(z, p)
            oa, s_t = _rwkv_sample(z, state_a_shift[0], jnp.transpose(state_a_wkv[0], (1, 2, 3, 0)), p, oa)
            new_s['a_wkv'] = jnp.transpose(s_t, (3, 0, 1, 2))
            ob, new_p['b_delta'] = _gdn_prompt(z, p)
            ob, new_s['b_delta'] = _gdn_sample(z, state_b_conv[0], state_b_delta[0], p, ob)
            o = _matmul([oa, ob], w_out_even, 0, name="w_out_even")
            tails = _prompt_tails(z)
            new_p['a_shift'] = tails[:, 7, :A_COLS]
            new_s['a_shift'] = z[NP_ROWS:, :A_COLS]
            new_p['b_conv'] = tails[:, 8 - (CONV_W - 1):, A_COLS:A_COLS + 3 * B_WIDTH]
            new_s['b_conv'] = jnp.concatenate(
                [state_b_conv[0][:, 1:], z[NP_ROWS:, None, A_COLS:A_COLS + 3 * B_WIDTH]], axis=1)
        else:
            z = _matmul([h], w_in_odd, 0, name="w_in_odd")
            oc, hp = _rglru_prompt(z, p)
            new_p['c_h'] = hp.reshape(BATCH, C_WIDTH)
            oc, new_s['c_h'] = _rglru_sample(z, state_c_conv[0], state_c_h[0], p, oc)
            od, new_p['d_s'] = _hgrn_prompt(z, p, layer)
            od, new_s['d_s'] = _hgrn_sample(z, state_d_s[0], p, layer, od)
            o = _matmul([oc, od], w_out_odd, 0, name="w_out_odd")
            new_p['c_conv'] = _prompt_tails(z)[:, 8 - (CONV_W - 1):, :C_WIDTH]
            new_s['c_conv'] = jnp.concatenate([state_c_conv[0][:, 1:], z[NP_ROWS:, None, :C_WIDTH]], axis=1)
        x_mid, h_ffn, route = _mid(xp, xs, split, o, mod, norm_ffn, w_router, b_router, layer)
        tile_e, tile_src, tile_rows, n_valid, order_tok, pos = _dispatch(route)
        y_sorted = _experts(tile_e, tile_src, tile_rows, n_valid, order_tok, h_ffn, w_expert_gate,
                            w_expert_up, w_expert_down, layer)
        if layer + 1 < DEPTH:
            x_all, h = _combine(pos, x_mid, route, mod, y_sorted, norm_mix.reshape(DEPTH, 1, D_MODEL),
                                layer + 1, mods[layer + 1], final=False)
            xp = xs = x_all
            split = False
        else:
            y_p, y_s = _combine(pos, x_mid, route, mod, y_sorted, final_norm.reshape(1, 1, D_MODEL), 0,
                                mod, final=True)
    st = lambda d, key: d[key][None]
    return (y_p.reshape(BATCH, SEQ, D_MODEL), y_s.reshape(DEC_BATCH, 1, D_MODEL),
            st(new_p, 'a_shift'), st(new_p, 'a_wkv'), st(new_p, 'b_conv'), st(new_p, 'b_delta'),
            st(new_p, 'c_conv'), st(new_p, 'c_h'), st(new_p, 'd_s'),
            st(new_s, 'a_shift'), st(new_s, 'a_wkv'), st(new_s, 'b_conv'), st(new_s, 'b_delta'),
            st(new_s, 'c_conv'), st(new_s, 'c_h'), st(new_s, 'd_s'))
```
